```python
import math
import jax, jax.numpy as jnp
from jax import lax
import numpy as np

D_MODEL = 1024
BATCH = 4
SEQ = 4096
DEPTH = 2

GRID_W = 64
D_FF = 2816
NORM_EPS = 1e-6
N_BRANCHES = 3
S5_GROUPS = 32
S5_GROUP_CH = 16
S5_STATE = 64
S5_WIDTH = S5_GROUPS * S5_GROUP_CH
S5_DT_MIN = 1e-3
S5_DT_MAX = 1e-1
GLA_HEADS = 4
GLA_HEAD_DIM = 128
GLA_WIDTH = GLA_HEADS * GLA_HEAD_DIM
GLA_LOWRANK = 16
GLA_TAU = 16.0
GLA_CHUNK = 64
ATTN_Q_HEADS = 8
ATTN_KV_HEADS = 2
ATTN_HEAD_DIM = 64
ATTN_WIDTH = ATTN_Q_HEADS * ATTN_HEAD_DIM
ATTN_KV_WIDTH = ATTN_KV_HEADS * ATTN_HEAD_DIM
ATTN_BLOCK = 128
ROPE_BASE = 10000.0
IN_SPLITS = (S5_WIDTH, GLA_WIDTH, GLA_WIDTH, GLA_WIDTH, GLA_WIDTH, GLA_LOWRANK, GLA_LOWRANK, ATTN_WIDTH, ATTN_KV_WIDTH, ATTN_KV_WIDTH)
IN_WIDTH = sum(IN_SPLITS)

kernel_name = 'hybrid_s5_gla_gqa_macaron_encoder'

F32 = jnp.float32


def rms_norm(x, gain):
    x32 = x.astype(F32)
    y = x32 * lax.rsqrt(jnp.mean(x32 * x32, axis=-1, keepdims=True) + NORM_EPS)
    return (y * gain.astype(F32)).astype(x.dtype)


def swiglu_ffn(h, w_gate, w_up, w_down):
    return (jax.nn.silu(h @ w_gate) * (h @ w_up)) @ w_down


def _linear_recurrence(left, right):
    a_l, b_l = left
    a_r, b_r = right
    return a_r * a_l, a_r * b_l + b_r


def s5_scan_dir(u32, lam_re, lam_im, log_dt, b_re, b_im, c_re, c_im, reverse):
    lam = lax.complex(lam_re.astype(F32), lam_im.astype(F32))
    dt = jnp.exp(log_dt.astype(F32))[:, None]
    lam_bar = jnp.exp(lam * dt)
    b = lax.complex(b_re.astype(F32), b_im.astype(F32))
    b_bar = ((lam_bar - 1.0) / lam)[..., None] * b
    bu = jnp.einsum('blgh,gph->blgp', u32.astype(jnp.complex64), b_bar)
    a = jnp.broadcast_to(lam_bar, bu.shape)
    _, states = lax.associative_scan(_linear_recurrence, (a, bu), axis=1, reverse=reverse)
    c = lax.complex(c_re.astype(F32), c_im.astype(F32))
    return jnp.real(jnp.einsum('blgp,ghp->blgh', states, c))


def s5_branch(u, lam_re, lam_im, log_dt, b_re, b_im, c_re, c_im, d_skip, w_glu):
    bsz, seq_len, _ = u.shape
    u32 = u.astype(F32)
    ug = u32.reshape(bsz, seq_len, S5_GROUPS, S5_GROUP_CH)
    y = (s5_scan_dir(ug, lam_re[0], lam_im[0], log_dt[0], b_re[0], b_im[0], c_re[0], c_im[0], False)
         + s5_scan_dir(ug, lam_re[1], lam_im[1], log_dt[1], b_re[1], b_im[1], c_re[1], c_im[1], True))
    y = y.reshape(bsz, seq_len, S5_WIDTH) + d_skip.astype(F32) * u32
    y = jax.nn.gelu(y).astype(u.dtype)
    return y * jax.nn.sigmoid(y @ w_glu)


def gla_chunked(q, k, v, log_a):
    bsz, seq_len, nh, dk = q.shape
    dv = v.shape[-1]
    n_chunks = seq_len // GLA_CHUNK

    def chunks(t):
        return t.reshape(bsz, n_chunks, GLA_CHUNK, nh, t.shape[-1])

    q, k, v, log_a = chunks(q), chunks(k), chunks(v), chunks(log_a)
    b = jnp.cumsum(log_a, axis=2)
    b_last = b[:, :, -1]
    q_dec = q * jnp.exp(b)
    k_dec = k * jnp.exp(-b)
    mask = jnp.tril(jnp.ones((GLA_CHUNK, GLA_CHUNK), dtype=bool))
    scores = jnp.where(mask, jnp.einsum('bnihd,bnjhd->bnhij', q_dec, k_dec), 0.0)
    o_intra = jnp.einsum('bnhij,bnjhe->bnihe', scores, v)
    k_to_end = k * jnp.exp(b_last[:, :, None] - b)
    chunk_kv = jnp.einsum('bnjhd,bnjhe->nbhde', k_to_end, v)
    chunk_decay = jnp.exp(jnp.moveaxis(b_last, 1, 0))

    def step(state, inp):
        decay, kv = inp
        return decay[..., None] * state + kv, state

    init = jnp.zeros((bsz, nh, dk, dv), F32)
    _, prev_states = lax.scan(step, init, (chunk_decay, chunk_kv))
    o_inter = jnp.einsum('bnihd,nbhde->bnihe', q_dec, prev_states)
    return (o_intra + o_inter).reshape(bsz, seq_len, nh, dv)


def gla_branch(q, k, v, gate, z_f, z_b, w_alpha, b_alpha, norm_gain):
    bsz, seq_len, _ = q.shape

    def heads(t):
        return t.astype(F32).reshape(bsz, seq_len, GLA_HEADS, GLA_HEAD_DIM)

    qh = heads(q) * GLA_HEAD_DIM ** -0.5
    kh = heads(k)
    vh = heads(v)

    def log_gate(z, w, bias):
        logits = (z @ w + bias).astype(F32)
        return heads(jax.nn.log_sigmoid(logits) / GLA_TAU)

    la_f = log_gate(z_f, w_alpha[0], b_alpha[0])
    la_b = log_gate(z_b, w_alpha[1], b_alpha[1])

    def flip(t):
        return jnp.flip(t, axis=1)

    o_f = gla_chunked(qh, kh, vh, la_f)
    o_b = flip(gla_chunked(flip(qh), flip(kh), flip(vh), flip(la_b)))
    o = rms_norm(o_f + o_b, norm_gain).reshape(bsz, seq_len, GLA_WIDTH).astype(q.dtype)
    return o * jax.nn.silu(gate)


def rope_1d(x, pos):
    d = x.shape[-1]
    half = d // 2
    inv_freq = ROPE_BASE ** (-jnp.arange(half, dtype=F32) * 2.0 / d)
    ang = pos.astype(F32)[:, None] * inv_freq[None, :]
    cos = jnp.cos(ang)[:, None, :]
    sin = jnp.sin(ang)[:, None, :]
    x1, x2 = x[..., :half], x[..., half:]
    return jnp.concatenate([x1 * cos - x2 * sin, x2 * cos + x1 * sin], axis=-1)


def axial_rope(x, rows, cols):
    half = x.shape[-1] // 2
    return jnp.concatenate([rope_1d(x[..., :half], rows), rope_1d(x[..., half:], cols)], axis=-1)


def attn_branch(q, k, v, q_gain, k_gain):
    bsz, seq_len, _ = q.shape
    n_rows = seq_len // GRID_W
    rows = jnp.repeat(jnp.arange(n_rows, dtype=jnp.int32), GRID_W)
    cols = jnp.tile(jnp.arange(GRID_W, dtype=jnp.int32), n_rows)
    qh = q.astype(F32).reshape(bsz, seq_len, ATTN_Q_HEADS, ATTN_HEAD_DIM)
    kh = k.astype(F32).reshape(bsz, seq_len, ATTN_KV_HEADS, ATTN_HEAD_DIM)
    vh = v.astype(F32).reshape(bsz, seq_len, ATTN_KV_HEADS, ATTN_HEAD_DIM)
    qh = axial_rope(rms_norm(qh, q_gain), rows, cols) * ATTN_HEAD_DIM ** -0.5
    kh = axial_rope(rms_norm(kh, k_gain), rows, cols)
    group = ATTN_Q_HEADS // ATTN_KV_HEADS
    n_blocks = seq_len // ATTN_BLOCK
    q_blocks = qh.reshape(bsz, n_blocks, ATTN_BLOCK, ATTN_KV_HEADS, group, ATTN_HEAD_DIM)
    q_blocks = jnp.moveaxis(q_blocks, 1, 0)

    def attend(qb):
        s = jnp.einsum('bqkgd,bskd->bkgqs', qb, kh)
        p = jax.nn.softmax(s, axis=-1)
        return jnp.einsum('bkgqs,bskd->bqkgd', p, vh)

    out = lax.map(attend, q_blocks)
    out = jnp.moveaxis(out, 0, 1).reshape(bsz, seq_len, ATTN_WIDTH)
    return out.astype(q.dtype)


def setup_inputs(seed: int = 0) -> dict:
    key = jax.random.key(seed)
    keys = iter(jax.random.split(key, 40))

    def normal(shape, scale):
        return scale * jax.random.normal(next(keys), shape, F32)

    def gain(shape):
        return 1.0 + normal(shape, 0.02)

    G, H, P = S5_GROUPS, S5_GROUP_CH, S5_STATE
    x = normal((BATCH, SEQ, D_MODEL), 1.0)
    ffn1_norm = gain((DEPTH, D_MODEL))
    ffn1_w_gate = normal((DEPTH, D_MODEL, D_FF), D_MODEL ** -0.5)
    ffn1_w_up = normal((DEPTH, D_MODEL, D_FF), D_MODEL ** -0.5)
    ffn1_w_down = normal((DEPTH, D_FF, D_MODEL), D_FF ** -0.5)
    mix_norm = gain((DEPTH, D_MODEL))
    w_in = normal((DEPTH, D_MODEL, IN_WIDTH), D_MODEL ** -0.5)
    s5_lambda_re = -0.5 + normal((DEPTH, 2, G, P), 0.01)
    s5_lambda_im = math.pi * jnp.arange(P, dtype=F32) + normal((DEPTH, 2, G, P), 0.01)
    s5_log_dt = jax.random.uniform(next(keys), (DEPTH, 2, G), F32, math.log(S5_DT_MIN), math.log(S5_DT_MAX))
    s5_b_re = normal((DEPTH, 2, G, P, H), (0.5 / H) ** 0.5)
    s5_b_im = normal((DEPTH, 2, G, P, H), (0.5 / H) ** 0.5)
    s5_c_re = normal((DEPTH, 2, G, H, P), (0.5 / P) ** 0.5)
    s5_c_im = normal((DEPTH, 2, G, H, P), (0.5 / P) ** 0.5)
    s5_d = normal((DEPTH, S5_WIDTH), 1.0)
    s5_w_glu = normal((DEPTH, S5_WIDTH, S5_WIDTH), S5_WIDTH ** -0.5)
    gla_w_alpha = normal((DEPTH, 2, GLA_LOWRANK, GLA_WIDTH), GLA_LOWRANK ** -0.5)
    gla_b_alpha = normal((DEPTH, 2, GLA_WIDTH), 0.1)
    gla_norm = gain((DEPTH, GLA_HEAD_DIM))
    attn_q_norm = gain((DEPTH, ATTN_HEAD_DIM))
    attn_k_norm = gain((DEPTH, ATTN_HEAD_DIM))
    w_branch_s5 = normal((DEPTH, S5_WIDTH, D_MODEL), S5_WIDTH ** -0.5)
    w_branch_gla = normal((DEPTH, GLA_WIDTH, D_MODEL), GLA_WIDTH ** -0.5)
    w_branch_attn = normal((DEPTH, ATTN_WIDTH, D_MODEL), ATTN_WIDTH ** -0.5)
    w_merge_gate = normal((DEPTH, D_MODEL, N_BRANCHES * D_MODEL), D_MODEL ** -0.5)
    b_merge_gate = normal((DEPTH, N_BRANCHES * D_MODEL), 0.01)
    w_out = normal((DEPTH, D_MODEL, D_MODEL), D_MODEL ** -0.5)
    ffn2_norm = gain((DEPTH, D_MODEL))
    ffn2_w_gate = normal((DEPTH, D_MODEL, D_FF), D_MODEL ** -0.5)
    ffn2_w_up = normal((DEPTH, D_MODEL, D_FF), D_MODEL ** -0.5)
    ffn2_w_down = normal((DEPTH, D_FF, D_MODEL), D_FF ** -0.5)
    final_norm = gain((D_MODEL,))
    return {'x': x, 'ffn1_norm': ffn1_norm, 'ffn1_w_gate': ffn1_w_gate, 'ffn1_w_up': ffn1_w_up,
            'ffn1_w_down': ffn1_w_down, 'mix_norm': mix_norm, 'w_in': w_in,
            's5_lambda_re': s5_lambda_re, 's5_lambda_im': s5_lambda_im, 's5_log_dt': s5_log_dt,
            's5_b_re': s5_b_re, 's5_b_im': s5_b_im, 's5_c_re': s5_c_re, 's5_c_im': s5_c_im,
            's5_d': s5_d, 's5_w_glu': s5_w_glu, 'gla_w_alpha': gla_w_alpha, 'gla_b_alpha': gla_b_alpha,
            'gla_norm': gla_norm, 'attn_q_norm': attn_q_norm, 'attn_k_norm': attn_k_norm,
            'w_branch_s5': w_branch_s5, 'w_branch_gla': w_branch_gla, 'w_branch_attn': w_branch_attn,
            'w_merge_gate': w_merge_gate, 'b_merge_gate': b_merge_gate, 'w_out': w_out,
            'ffn2_norm': ffn2_norm, 'ffn2_w_gate': ffn2_w_gate, 'ffn2_w_up': ffn2_w_up,
            'ffn2_w_down': ffn2_w_down, 'final_norm': final_norm}


def reference(x, ffn1_norm, ffn1_w_gate, ffn1_w_up, ffn1_w_down, mix_norm, w_in,
              s5_lambda_re, s5_lambda_im, s5_log_dt, s5_b_re, s5_b_im, s5_c_re, s5_c_im,
              s5_d, s5_w_glu, gla_w_alpha, gla_b_alpha, gla_norm, attn_q_norm, attn_k_norm,
              w_branch_s5, w_branch_gla, w_branch_attn, w_merge_gate, b_merge_gate, w_out,
              ffn2_norm, ffn2_w_gate, ffn2_w_up, ffn2_w_down, final_norm):
    bsz, seq_len, _ = x.shape
    split_at = [int(c) for c in np.cumsum(IN_SPLITS)[:-1]]
    for i in range(DEPTH):
        h = rms_norm(x, ffn1_norm[i])
        x = x + 0.5 * swiglu_ffn(h, ffn1_w_gate[i], ffn1_w_up[i], ffn1_w_down[i])

        h = rms_norm(x, mix_norm[i])
        (s5_u, gla_q, gla_k, gla_v, gla_g, gla_zf, gla_zb,
         at_q, at_k, at_v) = jnp.split(h @ w_in[i], split_at, axis=-1)
        y_s5 = s5_branch(s5_u, s5_lambda_re[i], s5_lambda_im[i], s5_log_dt[i], s5_b_re[i], s5_b_im[i],
                         s5_c_re[i], s5_c_im[i], s5_d[i], s5_w_glu[i])
        y_gla = gla_branch(gla_q, gla_k, gla_v, gla_g, gla_zf, gla_zb,
                           gla_w_alpha[i], gla_b_alpha[i], gla_norm[i])
        y_attn = attn_branch(at_q, at_k, at_v, attn_q_norm[i], attn_k_norm[i])
        gates = jax.nn.sigmoid(h @ w_merge_gate[i] + b_merge_gate[i])
        gates = gates.reshape(bsz, seq_len, N_BRANCHES, D_MODEL)
        merged = (gates[:, :, 0] * (y_s5 @ w_branch_s5[i])
                  + gates[:, :, 1] * (y_gla @ w_branch_gla[i])
                  + gates[:, :, 2] * (y_attn @ w_branch_attn[i]))
        x = x + merged @ w_out[i]

        h = rms_norm(x, ffn2_norm[i])
        x = x + 0.5 * swiglu_ffn(h, ffn2_w_gate[i], ffn2_w_up[i], ffn2_w_down[i])
    return rms_norm(x, final_norm)
```

```python
import functools

import jax
import jax.numpy as jnp
import numpy as np
from jax import lax
from jax.experimental import pallas as pl
from jax.experimental.pallas import tpu as pltpu

F32 = jnp.float32
BF16 = jnp.bfloat16

NORM_EPS = 1e-6
S5_GROUPS = 32
S5_GROUP_CH = 16
S5_STATE = 64
S5_WIDTH = S5_GROUPS * S5_GROUP_CH
S5_CHUNK = 8
S5_LANE_GROUPS = 8
GLA_HEADS = 4
GLA_HEAD_DIM = 128
GLA_WIDTH = GLA_HEADS * GLA_HEAD_DIM
GLA_LOWRANK = 16
GLA_TAU = 16.0
GLA_CHUNK = 64
ATTN_Q_HEADS = 8
ATTN_KV_HEADS = 2
ATTN_HEAD_DIM = 64
ATTN_GROUP = ATTN_Q_HEADS // ATTN_KV_HEADS
ATTN_WIDTH = ATTN_Q_HEADS * ATTN_HEAD_DIM
ATTN_KV_WIDTH = ATTN_KV_HEADS * ATTN_HEAD_DIM
GRID_W = 64
ROPE_BASE = 10000.0
ROPE_PAIR = ATTN_HEAD_DIM // 4

LANE = 128
V7X_VMEM_BYTES = 64 * 1024 * 1024

COL_S5 = 0
COL_GQ = 512
COL_GK = 1024
COL_GV = 1536
COL_GG = 2048
COL_AQ = 2560
COL_AKV = 3072
COL_Z = 3328
PROJ_WIDTH = 3456


def _cparams(semantics, vmem_mib):
    return pltpu.CompilerParams(
        dimension_semantics=semantics,
        vmem_limit_bytes=min(vmem_mib * 1024 * 1024, V7X_VMEM_BYTES - 4 * 1024 * 1024),
    )


def _resident(shape):
    nd = len(shape)
    return pl.BlockSpec(shape, lambda *_: (0,) * nd, pipeline_mode=pl.Buffered(1))


def _rms(x, gain):
    ms = jnp.mean(x * x, axis=-1, keepdims=True)
    return x * lax.rsqrt(ms + NORM_EPS) * gain


def _dot(a, b):
    return jnp.dot(a, b, preferred_element_type=F32)


def _dot_nt(a, b):
    return lax.dot_general(a, b, (((1,), (1,)), ((), ())), preferred_element_type=F32)


def _dot_tn(a, b):
    return lax.dot_general(a, b, (((0,), (0,)), ((), ())), preferred_element_type=F32)


def _ffn_body(x_ref, gain_ref, wg_ref, wu_ref, wd_ref, *rest, chunk, final):
    if final:
        fg_ref, o_ref, a_ref = rest
    else:
        o_ref, a_ref = rest
    x = x_ref[...]
    h = _rms(x, gain_ref[...]).astype(BF16)
    d_ff = wg_ref.shape[1]
    for c0 in range(0, d_ff, chunk):
        g = _dot(h, wg_ref[:, c0:c0 + chunk])
        u = _dot(h, wu_ref[:, c0:c0 + chunk])
        a_ref[:, c0:c0 + chunk] = (g * jax.nn.sigmoid(g) * u).astype(BF16)
    out = x + 0.5 * _dot(a_ref[...], wd_ref[...])
    if final:
        out = _rms(out, fg_ref[...])
    o_ref[...] = out


def _ffn(x2d, gain, wg, wu, wd, final_gain=None, *, tm=512, chunk=256):
    t, d = x2d.shape
    f = wg.shape[1]
    final = final_gain is not None
    row = pl.BlockSpec((tm, d), lambda i: (i, 0))
    in_specs = [row, _resident((1, d)), _resident((d, f)), _resident((d, f)), _resident((f, d))]
    args = [x2d, gain.reshape(1, d), wg, wu, wd]
    if final:
        in_specs.append(_resident((1, d)))
        args.append(final_gain.reshape(1, d))
    return pl.pallas_call(
        functools.partial(_ffn_body, chunk=chunk, final=final),
        grid=(t // tm,),
        in_specs=in_specs,
        out_specs=row,
        out_shape=jax.ShapeDtypeStruct((t, d), F32),
        scratch_shapes=[pltpu.VMEM((tm, f), BF16)],
        compiler_params=_cparams(("parallel",), 48),
        name="ffn",
    )(*args)


def _inproj_body(x_ref, gain_ref, w_ref, o_ref, *, chunk):
    h = _rms(x_ref[...], gain_ref[...]).astype(BF16)
    n = w_ref.shape[1]
    for c0 in range(0, n, chunk):
        c1 = min(c0 + chunk, n)
        o_ref[:, c0:c1] = _dot(h, w_ref[:, c0:c1])


def _inproj(x2d, gain, w, *, tm=512, chunk=512):
    t, d = x2d.shape
    n = w.shape[1]
    return pl.pallas_call(
        functools.partial(_inproj_body, chunk=chunk),
        grid=(t // tm,),
        in_specs=[pl.BlockSpec((tm, d), lambda i: (i, 0)), _resident((1, d)), _resident((d, n))],
        out_specs=pl.BlockSpec((tm, n), lambda i: (i, 0)),
        out_shape=jax.ShapeDtypeStruct((t, n), F32),
        compiler_params=_cparams(("parallel",), 40),
        name="inproj",
    )(x2d, gain.reshape(1, d), w)


def _head_rms(x, ones_ref, gain):
    x2 = x * x
    hi = x2.astype(BF16)
    lo = (x2 - hi.astype(F32)).astype(BF16)
    ss = _dot(hi, ones_ref[...]) + _dot(lo, ones_ref[...])
    return x * lax.rsqrt(ss * (1.0 / ATTN_HEAD_DIM) + NORM_EPS) * gain


def _rope(x, cos, sin_signed):
    n = x.shape[-1]
    lane = lax.broadcasted_iota(jnp.int32, x.shape, 1)
    first = (lane & ROPE_PAIR) == 0
    partner = jnp.where(first, pltpu.roll(x, n - ROPE_PAIR, 1), pltpu.roll(x, ROPE_PAIR, 1))
    return x * cos + partner * sin_signed


def _attn_prep_body(q_ref, kv_ref, cq_ref, sq_ref, ck_ref, sk_ref, gq_ref, gk_ref, oq_ref, ok_ref,
                    qo_ref, kt_ref, v_ref):
    q = _rope(_head_rms(q_ref[0], oq_ref, gq_ref[...]), cq_ref[...], sq_ref[...])
    qo_ref[0] = q.astype(BF16)
    kv = kv_ref[0]
    k = _rope(_head_rms(kv[:, :ATTN_KV_WIDTH], ok_ref, gk_ref[...]), ck_ref[...], sk_ref[...])
    kt_ref[0] = k.T.astype(BF16)
    for g in range(ATTN_KV_HEADS):
        lo = ATTN_KV_WIDTH + g * ATTN_HEAD_DIM
        v_ref[0, g] = kv[:, lo:lo + ATTN_HEAD_DIM].astype(BF16)


def _attn_prep(proj, tabs, q_gain, k_gain, *, tm=512):
    b, l, _ = proj.shape
    cq, sq, ck, sk = tabs
    ones_q = jnp.asarray(np.kron(np.eye(ATTN_Q_HEADS), np.ones((ATTN_HEAD_DIM, ATTN_HEAD_DIM))), BF16)
    ones_k = jnp.asarray(np.kron(np.eye(ATTN_KV_HEADS), np.ones((ATTN_HEAD_DIM, ATTN_HEAD_DIM))), BF16)
    gq = jnp.tile(q_gain, ATTN_Q_HEADS).reshape(1, ATTN_WIDTH)
    gk = jnp.tile(k_gain, ATTN_KV_HEADS).reshape(1, ATTN_KV_WIDTH)
    kvw = 2 * ATTN_KV_WIDTH
    tab_q = pl.BlockSpec((tm, ATTN_WIDTH), lambda bi, i: (i, 0))
    tab_k = pl.BlockSpec((tm, ATTN_KV_WIDTH), lambda bi, i: (i, 0))
    return pl.pallas_call(
        _attn_prep_body,
        grid=(b, l // tm),
        in_specs=[
            pl.BlockSpec((1, tm, ATTN_WIDTH), lambda bi, i: (bi, i, COL_AQ // ATTN_WIDTH)),
            pl.BlockSpec((1, tm, kvw), lambda bi, i: (bi, i, COL_AKV // kvw)),
            tab_q, tab_q, tab_k, tab_k,
            _resident((1, ATTN_WIDTH)), _resident((1, ATTN_KV_WIDTH)),
            _resident((ATTN_WIDTH, ATTN_WIDTH)), _resident((ATTN_KV_WIDTH, ATTN_KV_WIDTH)),
        ],
        out_specs=[
            pl.BlockSpec((1, tm, ATTN_WIDTH), lambda bi, i: (bi, i, 0)),
            pl.BlockSpec((1, ATTN_KV_WIDTH, tm), lambda bi, i: (bi, 0, i)),
            pl.BlockSpec((1, ATTN_KV_HEADS, tm, ATTN_HEAD_DIM), lambda bi, i: (bi, 0, i, 0)),
        ],
        out_shape=[
            jax.ShapeDtypeStruct((b, l, ATTN_WIDTH), BF16),
            jax.ShapeDtypeStruct((b, ATTN_KV_WIDTH, l), BF16),
            jax.ShapeDtypeStruct((b, ATTN_KV_HEADS, l, ATTN_HEAD_DIM), BF16),
        ],
        compiler_params=_cparams(("parallel", "parallel"), 32),
        name="attn_prep",
    )(proj, proj, cq, sq, ck, sk, gq, gk, ones_q, ones_k)


def _rope_tables(l):
    t = jnp.arange(l, dtype=jnp.int32)
    rows = (t // GRID_W).astype(F32)
    cols = (t % GRID_W).astype(F32)
    half = ATTN_HEAD_DIM // 2
    inv_freq = ROPE_BASE ** (-jnp.arange(half // 2, dtype=F32) * 2.0 / half)
    d = np.arange(ATTN_HEAD_DIM)
    use_rows = jnp.asarray((d // half) == 0)
    freq = inv_freq[jnp.asarray(d % (half // 2))]
    second = jnp.asarray((d % half) >= half // 2)
    ang = jnp.where(use_rows[None, :], rows[:, None], cols[:, None]) * freq[None, :]
    cos = jnp.cos(ang)
    sin = jnp.sin(ang)
    sin = jnp.where(second[None, :], sin, -sin)
    scale = ATTN_HEAD_DIM ** -0.5
    return (jnp.tile(cos, (1, ATTN_Q_HEADS)) * scale, jnp.tile(sin, (1, ATTN_Q_HEADS)) * scale,
            jnp.tile(cos, (1, ATTN_KV_HEADS)), jnp.tile(sin, (1, ATTN_KV_HEADS)))


def _attn_body(q_ref, kt_ref, v_ref, o_ref):
    kt = kt_ref[0]
    v = v_ref[0, 0]
    outs = []
    for h in range(ATTN_GROUP):
        qh = q_ref[0, :, h * ATTN_HEAD_DIM:(h + 1) * ATTN_HEAD_DIM]
        s = _dot(qh, kt)
        p = jnp.exp(s - jnp.max(s, axis=-1, keepdims=True))
        denom = jnp.sum(p, axis=-1, keepdims=True)
        outs.append(_dot(p.astype(BF16), v) / denom)
    o_ref[0] = jnp.concatenate(outs, axis=-1).astype(o_ref.dtype)


def _attention(q, kt, v, *, tq=256):
    b, l, _ = q.shape
    gw = ATTN_GROUP * ATTN_HEAD_DIM
    return pl.pallas_call(
        _attn_body,
        grid=(b, ATTN_KV_HEADS, l // tq),
        in_specs=[
            pl.BlockSpec((1, tq, gw), lambda bi, g, i: (bi, i, g)),
            pl.BlockSpec((1, ATTN_HEAD_DIM, l), lambda bi, g, i: (bi, g, 0)),
            pl.BlockSpec((1, 1, l, ATTN_HEAD_DIM), lambda bi, g, i: (bi, g, 0, 0)),
        ],
        out_specs=pl.BlockSpec((1, tq, gw), lambda bi, g, i: (bi, i, g)),
        out_shape=jax.ShapeDtypeStruct((b, l, ATTN_WIDTH), BF16),
        compiler_params=_cparams(("parallel", "parallel", "parallel"), 48),
        name="attention",
    )(q, kt, v)


def _split3(x):
    h1 = x.astype(BF16)
    r1 = x - h1.astype(F32)
    h2 = r1.astype(BF16)
    h3 = (r1 - h2.astype(F32)).astype(BF16)
    return h1, h2, h3


def _gla_direction(q, k, v, z, wa, ba, cum_ref, ones_ref, st_ref, reverse):
    tb = q.shape[0]
    n_chunks = tb // GLA_CHUNK
    logits = _dot(z.astype(BF16), wa) + ba
    log_a = (jnp.minimum(logits, 0.0) - jnp.log1p(jnp.exp(-jnp.abs(logits)))) * (1.0 / GLA_TAU)
    parts = _split3(log_a)
    cum = cum_ref[...]
    ones = ones_ref[...]
    bcum = _dot(cum, parts[0]) + _dot(cum, parts[1]) + _dot(cum, parts[2])
    btot = _dot(ones, parts[0]) + _dot(ones, parts[1]) + _dot(ones, parts[2])
    q_dec = ((q * GLA_HEAD_DIM ** -0.5) * jnp.exp(bcum)).astype(BF16)
    k_dec = (k * jnp.exp(-bcum)).astype(BF16)
    k_end = (k * jnp.exp(btot - bcum)).astype(BF16)
    decay = jnp.exp(btot)
    vb = v.astype(BF16)
    r = lax.broadcasted_iota(jnp.int32, (tb, tb), 0)
    c = lax.broadcasted_iota(jnp.int32, (tb, tb), 1)
    same = (r // GLA_CHUNK) == (c // GLA_CHUNK)
    causal = (c >= r) if reverse else (c <= r)
    scores = jnp.where(same & causal, _dot_nt(q_dec, k_dec), 0.0)
    o_intra = _dot(scores.astype(BF16), vb)
    st = st_ref[...]
    outs = [None] * n_chunks
    order = range(n_chunks - 1, -1, -1) if reverse else range(n_chunks)
    for ci in order:
        rows = slice(ci * GLA_CHUNK, (ci + 1) * GLA_CHUNK)
        outs[ci] = o_intra[rows] + _dot_nt(q_dec[rows], st.astype(BF16))
        st = decay[ci * GLA_CHUNK:ci * GLA_CHUNK + 1] * st + _dot_tn(vb[rows], k_end[rows])
    st_ref[...] = st
    return jnp.concatenate(outs, axis=0)


def _gla_body(qf_ref, kf_ref, vf_ref, zf_ref, qb_ref, kb_ref, vb_ref, zb_ref,
              waf_ref, wab_ref, baf_ref, bab_ref, cumf_ref, cumb_ref, ones_ref,
              of_ref, ob_ref, stf_ref, stb_ref):
    @pl.when(pl.program_id(2) == 0)
    def _():
        stf_ref[...] = jnp.zeros_like(stf_ref)
        stb_ref[...] = jnp.zeros_like(stb_ref)

    of_ref[0] = _gla_direction(qf_ref[0], kf_ref[0], vf_ref[0], zf_ref[0], waf_ref[0], baf_ref[0],
                               cumf_ref, ones_ref, stf_ref, False)
    ob_ref[0] = _gla_direction(qb_ref[0], kb_ref[0], vb_ref[0], zb_ref[0], wab_ref[0], bab_ref[0],
                               cumb_ref, ones_ref, stb_ref, True)


def _gla(proj, w_alpha, b_alpha, *, tb=512):
    b, l, _ = proj.shape
    nblk = l // tb
    hd = GLA_HEAD_DIM
    wa = w_alpha.reshape(2, GLA_LOWRANK, GLA_HEADS, hd).transpose(0, 2, 1, 3)
    waf = jnp.zeros((GLA_HEADS, LANE, hd), F32).at[:, :GLA_LOWRANK].set(wa[0]).astype(BF16)
    wab = jnp.zeros((GLA_HEADS, LANE, hd), F32).at[:, GLA_LOWRANK:2 * GLA_LOWRANK].set(wa[1]).astype(BF16)
    ba = b_alpha.reshape(2, GLA_HEADS, 1, hd)
    idx = np.arange(tb)
    same = (idx[:, None] // GLA_CHUNK) == (idx[None, :] // GLA_CHUNK)
    cum_f = jnp.asarray(same & (idx[None, :] <= idx[:, None]), BF16)
    cum_b = jnp.asarray(same & (idx[None, :] >= idx[:, None]), BF16)
    ones = jnp.asarray(same, BF16)

    def col(base, rev):
        if rev:
            return pl.BlockSpec((1, tb, hd), lambda bi, h, i: (bi, nblk - 1 - i, base // hd + h))
        return pl.BlockSpec((1, tb, hd), lambda bi, h, i: (bi, i, base // hd + h))

    def zcol(rev):
        if rev:
            return pl.BlockSpec((1, tb, LANE), lambda bi, h, i: (bi, nblk - 1 - i, COL_Z // LANE))
        return pl.BlockSpec((1, tb, LANE), lambda bi, h, i: (bi, i, COL_Z // LANE))

    per_head_w = pl.BlockSpec((1, LANE, hd), lambda bi, h, i: (h, 0, 0))
    per_head_b = pl.BlockSpec((1, 1, hd), lambda bi, h, i: (h, 0, 0))
    in_specs = [col(COL_GQ, False), col(COL_GK, False), col(COL_GV, False), zcol(False),
                col(COL_GQ, True), col(COL_GK, True), col(COL_GV, True), zcol(True),
                per_head_w, per_head_w, per_head_b, per_head_b,
                _resident((tb, tb)), _resident((tb, tb)), _resident((tb, tb))]
    out_f = pl.BlockSpec((1, tb, hd), lambda bi, h, i: (bi, i, h))
    out_b = pl.BlockSpec((1, tb, hd), lambda bi, h, i: (bi, nblk - 1 - i, h))
    return pl.pallas_call(
        _gla_body,
        grid=(b, GLA_HEADS, nblk),
        in_specs=in_specs,
        out_specs=[out_f, out_b],
        out_shape=[jax.ShapeDtypeStruct((b, l, GLA_WIDTH), F32)] * 2,
        scratch_shapes=[pltpu.VMEM((hd, hd), F32), pltpu.VMEM((hd, hd), F32)],
        compiler_params=_cparams(("parallel", "parallel", "arbitrary"), 32),
        name="gla",
    )(*([proj] * 8), waf, wab, ba[0], ba[1], cum_f, cum_b, ones)


def _s5_body(u_ref, tz_ref, win_ref, wout_ref, a_ref, d_ref, o_ref, z_scr, x_scr):
    l = u_ref.shape[1]
    nc = l // S5_CHUNK
    nst = S5_LANE_GROUPS * S5_STATE
    uk = jnp.concatenate([u_ref[0, pl.ds(t, nc, stride=S5_CHUNK), :] for t in range(S5_CHUNK)], axis=-1)
    ukb = uk.astype(BF16)
    z_scr[...] = _dot(ukb, win_ref[0])

    a = a_ref[0]
    afr, afi, abr, abi = (a[:, i * nst:(i + 1) * nst] for i in range(4))
    ntile = nc // 8

    def tile_step(i, carry):
        sfr, sfi, sbr, sbi = carry
        rf = pl.multiple_of(i * 8, 8)
        rb = pl.multiple_of((ntile - 1 - i) * 8, 8)
        zf = z_scr[pl.ds(rf, 8), 0:2 * nst]
        zb = z_scr[pl.ds(rb, 8), 2 * nst:4 * nst]
        xfr, xfi, xbr, xbi = [], [], [None] * 8, [None] * 8
        for r in range(8):
            xfr.append(sfr)
            xfi.append(sfi)
            sfr, sfi = (afr * sfr - afi * sfi + zf[r:r + 1, :nst],
                        afr * sfi + afi * sfr + zf[r:r + 1, nst:])
            q = 7 - r
            xbr[q] = sbr
            xbi[q] = sbi
            sbr, sbi = (abr * sbr - abi * sbi + zb[q:q + 1, :nst],
                        abr * sbi + abi * sbr + zb[q:q + 1, nst:])
        x_scr[pl.ds(rf, 8), 0:nst] = jnp.concatenate(xfr, axis=0)
        x_scr[pl.ds(rf, 8), nst:2 * nst] = jnp.concatenate(xfi, axis=0)
        x_scr[pl.ds(rb, 8), 2 * nst:3 * nst] = jnp.concatenate(xbr, axis=0)
        x_scr[pl.ds(rb, 8), 3 * nst:4 * nst] = jnp.concatenate(xbi, axis=0)
        return sfr, sfi, sbr, sbi

    zero = jnp.zeros((1, nst), F32)
    lax.fori_loop(0, ntile, tile_step, (zero, zero, zero, zero))

    y = _dot(ukb, tz_ref[0]) + _dot(x_scr[...].astype(BF16), wout_ref[0]) + uk * d_ref[0]
    for t in range(S5_CHUNK):
        o_ref[0, pl.ds(t, nc, stride=S5_CHUNK), :] = y[:, t * LANE:(t + 1) * LANE]


def _s5(proj, mats, d_skip):
    b, l, _ = proj.shape
    tz, win, wout, a = mats
    nk = S5_WIDTH // LANE
    nc = l // S5_CHUNK
    feat = S5_CHUNK * LANE
    nst4 = 4 * S5_LANE_GROUPS * S5_STATE
    d_tiled = jnp.tile(d_skip.reshape(nk, 1, LANE), (1, 1, S5_CHUNK))
    return pl.pallas_call(
        _s5_body,
        grid=(nk, b),
        in_specs=[
            pl.BlockSpec((1, l, LANE), lambda k, bi: (bi, 0, COL_S5 // LANE + k)),
            pl.BlockSpec((1, feat, feat), lambda k, bi: (k, 0, 0)),
            pl.BlockSpec((1, feat, nst4), lambda k, bi: (k, 0, 0)),
            pl.BlockSpec((1, nst4, feat), lambda k, bi: (k, 0, 0)),
            pl.BlockSpec((1, 1, nst4), lambda k, bi: (k, 0, 0)),
            pl.BlockSpec((1, 1, feat), lambda k, bi: (k, 0, 0)),
        ],
        out_specs=pl.BlockSpec((1, l, LANE), lambda k, bi: (bi, 0, k)),
        out_shape=jax.ShapeDtypeStruct((b, l, S5_WIDTH), F32),
        scratch_shapes=[pltpu.VMEM((nc, nst4), F32), pltpu.VMEM((nc, nst4), F32)],
        compiler_params=_cparams(("arbitrary", "arbitrary"), 48),
        name="s5",
    )(proj, tz, win, wout, a, d_tiled)


def _cmul(ar, ai, br, bi):
    return ar * br - ai * bi, ar * bi + ai * br


def _s5_matrices(lam_re, lam_im, log_dt, b_re, b_im, c_re, c_im):
    g, p, h, tc, lg = S5_GROUPS, S5_STATE, S5_GROUP_CH, S5_CHUNK, S5_LANE_GROUPS
    nk = g // lg
    dt = jnp.exp(log_dt)[..., None]
    mag = jnp.exp(lam_re * dt)
    lbr, lbi = mag * jnp.cos(lam_im * dt), mag * jnp.sin(lam_im * dt)
    den = lam_re * lam_re + lam_im * lam_im
    nr, ni = lbr - 1.0, lbi
    fr, fi = (nr * lam_re + ni * lam_im) / den, (ni * lam_re - nr * lam_im) / den
    bbr, bbi = _cmul(fr[..., None], fi[..., None], b_re, b_im)
    pwr, pwi = [jnp.ones_like(lbr)], [jnp.zeros_like(lbi)]
    for _ in range(tc):
        nr_, ni_ = _cmul(pwr[-1], pwi[-1], lbr, lbi)
        pwr.append(nr_)
        pwi.append(ni_)
    pwr, pwi = jnp.stack(pwr, 1), jnp.stack(pwi, 1)

    hi = lax.Precision.HIGHEST
    mr, mi = _cmul(c_re[:, None], c_im[:, None], pwr[:, :, :, None, :], pwi[:, :, :, None, :])
    kern = (jnp.einsum('djghp,dgpx->djghx', mr, bbr, precision=hi)
            - jnp.einsum('djghp,dgpx->djghx', mi, bbi, precision=hi))
    lag = np.arange(tc)[None, :] - np.arange(tc)[:, None]
    kf = kern[0][np.clip(lag, 0, tc - 1)]
    kb = kern[1][np.clip(-lag, 0, tc - 1)]
    lag_b = jnp.asarray(lag)[:, :, None, None, None]
    m = jnp.where(lag_b > 0, kf, 0.0) + jnp.where(lag_b < 0, kb, 0.0) + jnp.where(lag_b == 0, kf + kb, 0.0)
    m = m.transpose(2, 0, 4, 1, 3).reshape(nk, lg, tc, h, tc, h)
    eye = jnp.eye(lg, dtype=F32)
    tz = jnp.einsum('kgaxby,gq->kagxbqy', m, eye).reshape(nk, tc * lg * h, tc * lg * h)

    sel_f = np.arange(tc - 1, -1, -1)
    sel_b = np.arange(tc)
    wr, wi = [], []
    for d, sel in ((0, sel_f), (1, sel_b)):
        r_, i_ = _cmul(pwr[d][sel][..., None], pwi[d][sel][..., None], bbr[d][None], bbi[d][None])
        wr.append(r_)
        wi.append(i_)
    w = jnp.stack([jnp.stack([wr[0], wi[0]]), jnp.stack([wr[1], wi[1]])])
    w = w.transpose(3, 2, 5, 0, 1, 4).reshape(nk, lg, tc, h, 2, 2, p)
    win = jnp.einsum('kgaxdrp,gq->kagxdrqp', w, eye).reshape(nk, tc * lg * h, 4 * lg * p)

    sel_f = np.arange(1, tc + 1)
    sel_b = np.arange(tc, 0, -1)
    outs = []
    for d, sel in ((0, sel_f), (1, sel_b)):
        r_, i_ = _cmul(c_re[d][None], c_im[d][None], pwr[d][sel][:, :, None, :], pwi[d][sel][:, :, None, :])
        outs.append(jnp.stack([r_, -i_]))
    wo = jnp.stack(outs)
    wo = wo.transpose(3, 0, 1, 5, 2, 4).reshape(nk, lg, 2, 2, p, tc, h)
    wout = jnp.einsum('kgdrpby,gq->kdrgpbqy', wo, eye).reshape(nk, 4 * lg * p, tc * lg * h)

    a = jnp.stack([pwr[0, tc], pwi[0, tc], pwr[1, tc], pwi[1, tc]])
    a = a.reshape(4, nk, lg * p).transpose(1, 0, 2).reshape(nk, 1, 4 * lg * p)
    return tz.astype(BF16), win.astype(BF16), wout.astype(BF16), a


def _merge_body(x_ref, ys_ref, of_ref, ob_ref, gate_ref, ya_ref, mg_ref, wm_ref, bm_ref, wglu_ref,
                gg_ref, wbs_ref, wbg_ref, wba_ref, wo_ref, o_ref):
    x = x_ref[...]
    d = x.shape[1]
    h = _rms(x, mg_ref[...]).astype(BF16)

    y = jax.nn.gelu(ys_ref[...])
    y_s5 = y * jax.nn.sigmoid(_dot(y.astype(BF16), wglu_ref[...]))

    o = of_ref[...] + ob_ref[...]
    heads = []
    for hh in range(GLA_HEADS):
        oh = o[:, hh * GLA_HEAD_DIM:(hh + 1) * GLA_HEAD_DIM]
        heads.append(oh * lax.rsqrt(jnp.mean(oh * oh, axis=-1, keepdims=True) + NORM_EPS))
    gate = gate_ref[...]
    y_gla = jnp.concatenate(heads, axis=-1) * gg_ref[...] * (gate * jax.nn.sigmoid(gate))

    merged = None
    for i, (val, w_ref) in enumerate(((y_s5.astype(BF16), wbs_ref), (y_gla.astype(BF16), wbg_ref),
                                      (ya_ref[...], wba_ref))):
        g = jax.nn.sigmoid(_dot(h, wm_ref[:, i * d:(i + 1) * d]) + bm_ref[:, i * d:(i + 1) * d])
        term = g * _dot(val, w_ref[...])
        merged = term if merged is None else merged + term
    o_ref[...] = x + _dot(merged.astype(BF16), wo_ref[...])


def _merge(x2d, ys5, o_f, o_b, proj2d, y_attn, mix_gain, w_merge, b_merge, w_glu, gla_gain,
           wb_s5, wb_gla, wb_attn, w_out, *, tm=512):
    t, d = x2d.shape
    row = lambda w: pl.BlockSpec((tm, w), lambda i: (i, 0))
    return pl.pallas_call(
        _merge_body,
        grid=(t // tm,),
        in_specs=[
            row(d), row(S5_WIDTH), row(GLA_WIDTH), row(GLA_WIDTH),
            pl.BlockSpec((tm, GLA_WIDTH), lambda i: (i, COL_GG // GLA_WIDTH)),
            row(ATTN_WIDTH),
            _resident((1, d)), _resident(w_merge.shape), _resident((1, w_merge.shape[1])),
            _resident(w_glu.shape), _resident((1, GLA_WIDTH)),
            _resident(wb_s5.shape), _resident(wb_gla.shape), _resident(wb_attn.shape), _resident(w_out.shape),
        ],
        out_specs=row(d),
        out_shape=jax.ShapeDtypeStruct((t, d), F32),
        compiler_params=_cparams(("parallel",), 48),
        name="merge",
    )(x2d, ys5, o_f, o_b, proj2d, y_attn, mix_gain.reshape(1, d), w_merge, b_merge.reshape(1, -1), w_glu,
      jnp.tile(gla_gain, GLA_HEADS).reshape(1, GLA_WIDTH), wb_s5, wb_gla, wb_attn, w_out)


def _reorder_w_in(w_in):
    z0 = COL_AQ
    z1 = z0 + 2 * GLA_LOWRANK
    d = w_in.shape[0]
    pad = jnp.zeros((d, PROJ_WIDTH - w_in.shape[1]), w_in.dtype)
    return jnp.concatenate([w_in[:, :z0], w_in[:, z1:], w_in[:, z0:z1], pad], axis=1)


def kernel(x, ffn1_norm, ffn1_w_gate, ffn1_w_up, ffn1_w_down, mix_norm, w_in, s5_lambda_re, s5_lambda_im, s5_log_dt, s5_b_re, s5_b_im, s5_c_re, s5_c_im, s5_d, s5_w_glu, gla_w_alpha, gla_b_alpha, gla_norm, attn_q_norm, attn_k_norm, w_branch_s5, w_branch_gla, w_branch_attn, w_merge_gate, b_merge_gate, w_out, ffn2_norm, ffn2_w_gate, ffn2_w_up, ffn2_w_down, final_norm):
    bsz, seq_len, d_model = x.shape
    depth = w_in.shape[0]
    tabs = _rope_tables(seq_len)
    bf = lambda w: w.astype(BF16)
    x2d = x.reshape(bsz * seq_len, d_model)
    for i in range(depth):
        x2d = _ffn(x2d, ffn1_norm[i], bf(ffn1_w_gate[i]), bf(ffn1_w_up[i]), bf(ffn1_w_down[i]))

        proj2d = _inproj(x2d, mix_norm[i], bf(_reorder_w_in(w_in[i])))
        proj = proj2d.reshape(bsz, seq_len, PROJ_WIDTH)
        mats = _s5_matrices(s5_lambda_re[i], s5_lambda_im[i], s5_log_dt[i], s5_b_re[i], s5_b_im[i],
                            s5_c_re[i], s5_c_im[i])
        ys5 = _s5(proj, mats, s5_d[i])
        o_f, o_b = _gla(proj, gla_w_alpha[i], gla_b_alpha[i])
        q, kt, v = _attn_prep(proj, tabs, attn_q_norm[i], attn_k_norm[i])
        y_attn = _attention(q, kt, v)

        flat = lambda a: a.reshape(bsz * seq_len, a.shape[-1])
        x2d = _merge(x2d, flat(ys5), flat(o_f), flat(o_b), proj2d, flat(y_attn), mix_norm[i],
                     bf(w_merge_gate[i]), b_merge_gate[i], bf(s5_w_glu[i]), gla_norm[i],
                     bf(w_branch_s5[i]), bf(w_branch_gla[i]), bf(w_branch_attn[i]), bf(w_out[i]))

        last = i == depth - 1
        x2d = _ffn(x2d, ffn2_norm[i], bf(ffn2_w_gate[i]), bf(ffn2_w_up[i]), bf(ffn2_w_down[i]),
                   final_norm if last else None)
    return x2d.reshape(bsz, seq_len, d_model)
```

```python
import functools
import math

import jax
import jax.numpy as jnp
import numpy as np
from jax import lax
from jax.experimental import pallas as pl
from jax.experimental.pallas import tpu as pltpu

F32 = jnp.float32
BF16 = jnp.bfloat16

NORM_EPS = 1e-6
S5_GROUPS = 32
S5_GROUP_CH = 16
S5_STATE = 64
S5_WIDTH = S5_GROUPS * S5_GROUP_CH
S5_CHUNK = 8
S5_LANE_GROUPS = 8
GLA_HEADS = 4
GLA_HEAD_DIM = 128
GLA_WIDTH = GLA_HEADS * GLA_HEAD_DIM
GLA_LOWRANK = 16
GLA_TAU = 16.0
GLA_CHUNK = 64
ATTN_Q_HEADS = 8
ATTN_KV_HEADS = 2
ATTN_HEAD_DIM = 64
ATTN_GROUP = ATTN_Q_HEADS // ATTN_KV_HEADS
ATTN_WIDTH = ATTN_Q_HEADS * ATTN_HEAD_DIM
ATTN_KV_WIDTH = ATTN_KV_HEADS * ATTN_HEAD_DIM
GRID_W = 64
ROPE_BASE = 10000.0
ROPE_PAIR = ATTN_HEAD_DIM // 4

LANE = 128
V7X_VMEM_BYTES = 64 * 1024 * 1024

COL_S5 = 0
COL_GQ = 512
COL_GK = 1024
COL_GV = 1536
COL_GG = 2048
COL_AQ = 2560
COL_AKV = 3072
COL_Z = 3328
PROJ_WIDTH = 3456


def _cparams(semantics, vmem_mib):
    return pltpu.CompilerParams(
        dimension_semantics=semantics,
        vmem_limit_bytes=min(vmem_mib * 1024 * 1024, V7X_VMEM_BYTES - 4 * 1024 * 1024),
    )


def _resident(shape):
    nd = len(shape)
    return pl.BlockSpec(shape, lambda *_: (0,) * nd, pipeline_mode=pl.Buffered(1))


def _rms(x, gain):
    ms = jnp.mean(x * x, axis=-1, keepdims=True)
    return x * lax.rsqrt(ms + NORM_EPS) * gain


def _dot(a, b):
    return jnp.dot(a, b, preferred_element_type=F32)


def _dot_nt(a, b):
    return lax.dot_general(a, b, (((1,), (1,)), ((), ())), preferred_element_type=F32)


def _dot_tn(a, b):
    return lax.dot_general(a, b, (((0,), (0,)), ((), ())), preferred_element_type=F32)


def _ffn_body(x_ref, gain_ref, wg_ref, wu_ref, wd_ref, *rest, chunk, final):
    if final:
        fg_ref, o_ref, a_ref = rest
    else:
        o_ref, a_ref = rest
    x = x_ref[...]
    h = _rms(x, gain_ref[...]).astype(BF16)
    d_ff = wg_ref.shape[1]
    for c0 in range(0, d_ff, chunk):
        g = _dot(h, wg_ref[:, c0:c0 + chunk])
        u = _dot(h, wu_ref[:, c0:c0 + chunk])
        a_ref[:, c0:c0 + chunk] = (g * jax.nn.sigmoid(g) * u).astype(BF16)
    out = x + 0.5 * _dot(a_ref[...], wd_ref[...])
    if final:
        out = _rms(out, fg_ref[...])
    o_ref[...] = out


def _ffn(x2d, gain, wg, wu, wd, final_gain=None, *, tm=512, chunk=256):
    t, d = x2d.shape
    f = wg.shape[1]
    final = final_gain is not None
    row = pl.BlockSpec((tm, d), lambda i: (i, 0))
    in_specs = [row, _resident((1, d)), _resident((d, f)), _resident((d, f)), _resident((f, d))]
    args = [x2d, gain.reshape(1, d), wg, wu, wd]
    if final:
        in_specs.append(_resident((1, d)))
        args.append(final_gain.reshape(1, d))
    return pl.pallas_call(
        functools.partial(_ffn_body, chunk=chunk, final=final),
        grid=(t // tm,),
        in_specs=in_specs,
        out_specs=row,
        out_shape=jax.ShapeDtypeStruct((t, d), F32),
        scratch_shapes=[pltpu.VMEM((tm, f), BF16)],
        compiler_params=_cparams(("parallel",), 48),
        name="ffn",
    )(*args)


def _inproj_body(x_ref, gain_ref, w_ref, o_ref, *, chunk):
    h = _rms(x_ref[...], gain_ref[...]).astype(BF16)
    n = w_ref.shape[1]
    for c0 in range(0, n, chunk):
        c1 = min(c0 + chunk, n)
        o_ref[:, c0:c1] = _dot(h, w_ref[:, c0:c1])


def _inproj(x2d, gain, w, *, tm=512, chunk=512):
    t, d = x2d.shape
    n = w.shape[1]
    return pl.pallas_call(
        functools.partial(_inproj_body, chunk=chunk),
        grid=(t // tm,),
        in_specs=[pl.BlockSpec((tm, d), lambda i: (i, 0)), _resident((1, d)), _resident((d, n))],
        out_specs=pl.BlockSpec((tm, n), lambda i: (i, 0)),
        out_shape=jax.ShapeDtypeStruct((t, n), F32),
        compiler_params=_cparams(("parallel",), 40),
        name="inproj",
    )(x2d, gain.reshape(1, d), w)


def _head_rms(x, ones_ref, gain):
    x2 = x * x
    hi = x2.astype(BF16)
    lo = (x2 - hi.astype(F32)).astype(BF16)
    ss = _dot(hi, ones_ref[...]) + _dot(lo, ones_ref[...])
    return x * lax.rsqrt(ss * (1.0 / ATTN_HEAD_DIM) + NORM_EPS) * gain


def _rope(x, cos, sin_signed):
    n = x.shape[-1]
    lane = lax.broadcasted_iota(jnp.int32, x.shape, 1)
    first = (lane & ROPE_PAIR) == 0
    partner = jnp.where(first, pltpu.roll(x, n - ROPE_PAIR, 1), pltpu.roll(x, ROPE_PAIR, 1))
    return x * cos + partner * sin_signed


def _attn_prep_body(q_ref, kv_ref, cq_ref, sq_ref, ck_ref, sk_ref, gq_ref, gk_ref, oq_ref, ok_ref,
                    qt_ref, k_ref, vt_ref):
    q = _rope(_head_rms(q_ref[0], oq_ref, gq_ref[...]), cq_ref[...], sq_ref[...])
    qt_ref[0] = q.T.astype(BF16)
    kv = kv_ref[0]
    k = _rope(_head_rms(kv[:, :ATTN_KV_WIDTH], ok_ref, gk_ref[...]), ck_ref[...], sk_ref[...])
    for g in range(ATTN_KV_HEADS):
        k_ref[0, g] = k[:, g * ATTN_HEAD_DIM:(g + 1) * ATTN_HEAD_DIM].astype(BF16)
    vt_ref[0] = kv[:, ATTN_KV_WIDTH:].T.astype(BF16)


def _attn_prep(proj, tabs, q_gain, k_gain, *, tm=512):
    b, l, _ = proj.shape
    cq, sq, ck, sk = tabs
    ones_q = jnp.asarray(np.kron(np.eye(ATTN_Q_HEADS), np.ones((ATTN_HEAD_DIM, ATTN_HEAD_DIM))), BF16)
    ones_k = jnp.asarray(np.kron(np.eye(ATTN_KV_HEADS), np.ones((ATTN_HEAD_DIM, ATTN_HEAD_DIM))), BF16)
    gq = jnp.tile(q_gain, ATTN_Q_HEADS).reshape(1, ATTN_WIDTH)
    gk = jnp.tile(k_gain, ATTN_KV_HEADS).reshape(1, ATTN_KV_WIDTH)
    kvw = 2 * ATTN_KV_WIDTH
    tab_q = pl.BlockSpec((tm, ATTN_WIDTH), lambda bi, i: (i, 0))
    tab_k = pl.BlockSpec((tm, ATTN_KV_WIDTH), lambda bi, i: (i, 0))
    return pl.pallas_call(
        _attn_prep_body,
        grid=(b, l // tm),
        in_specs=[
            pl.BlockSpec((1, tm, ATTN_WIDTH), lambda bi, i: (bi, i, COL_AQ // ATTN_WIDTH)),
            pl.BlockSpec((1, tm, kvw), lambda bi, i: (bi, i, COL_AKV // kvw)),
            tab_q, tab_q, tab_k, tab_k,
            _resident((1, ATTN_WIDTH)), _resident((1, ATTN_KV_WIDTH)),
            _resident((ATTN_WIDTH, ATTN_WIDTH)), _resident((ATTN_KV_WIDTH, ATTN_KV_WIDTH)),
        ],
        out_specs=[
            pl.BlockSpec((1, ATTN_WIDTH, tm), lambda bi, i: (bi, 0, i)),
            pl.BlockSpec((1, ATTN_KV_HEADS, tm, ATTN_HEAD_DIM), lambda bi, i: (bi, 0, i, 0)),
            pl.BlockSpec((1, ATTN_KV_WIDTH, tm), lambda bi, i: (bi, 0, i)),
        ],
        out_shape=[
            jax.ShapeDtypeStruct((b, ATTN_WIDTH, l), BF16),
            jax.ShapeDtypeStruct((b, ATTN_KV_HEADS, l, ATTN_HEAD_DIM), BF16),
            jax.ShapeDtypeStruct((b, ATTN_KV_WIDTH, l), BF16),
        ],
        compiler_params=_cparams(("parallel", "parallel"), 32),
        name="attn_prep",
    )(proj, proj, cq, sq, ck, sk, gq, gk, ones_q, ones_k)


def _rope_tables(l):
    t = jnp.arange(l, dtype=jnp.int32)
    rows = (t // GRID_W).astype(F32)
    cols = (t % GRID_W).astype(F32)
    half = ATTN_HEAD_DIM // 2
    inv_freq = ROPE_BASE ** (-jnp.arange(half // 2, dtype=F32) * 2.0 / half)
    d = np.arange(ATTN_HEAD_DIM)
    use_rows = jnp.asarray((d // half) == 0)
    freq = inv_freq[jnp.asarray(d % (half // 2))]
    second = jnp.asarray((d % half) >= half // 2)
    ang = jnp.where(use_rows[None, :], rows[:, None], cols[:, None]) * freq[None, :]
    cos = jnp.cos(ang)
    sin = jnp.sin(ang)
    sin = jnp.where(second[None, :], sin, -sin)
    scale = ATTN_HEAD_DIM ** -0.5 * math.log2(math.e)
    return (jnp.tile(cos, (1, ATTN_Q_HEADS)) * scale, jnp.tile(sin, (1, ATTN_Q_HEADS)) * scale,
            jnp.tile(cos, (1, ATTN_KV_HEADS)), jnp.tile(sin, (1, ATTN_KV_HEADS)))


def _attn_body(qt_ref, k_ref, vt_ref, o_ref, *, kb):
    n_kb = k_ref.shape[2] // kb
    items = [(h, j) for h in range(ATTN_GROUP) for j in range(n_kb)]

    def scores(h, j):
        qt = qt_ref[0, h * ATTN_HEAD_DIM:(h + 1) * ATTN_HEAD_DIM, :]
        return _dot(k_ref[0, 0, j * kb:(j + 1) * kb, :], qt)

    outs = []
    st = scores(*items[0])
    run_max = run_sum = acc = None
    for idx, (h, j) in enumerate(items):
        st_next = scores(*items[idx + 1]) if idx + 1 < len(items) else None
        blk_max = jnp.max(st, axis=0, keepdims=True)
        new_max = blk_max if j == 0 else jnp.maximum(run_max, blk_max)
        p = jnp.exp2(st - new_max)
        blk_sum = jnp.sum(p, axis=0, keepdims=True)
        blk_acc = _dot(vt_ref[0, :, j * kb:(j + 1) * kb], p.astype(BF16))
        if j == 0:
            run_sum, acc = blk_sum, blk_acc
        else:
            alpha = jnp.exp2(run_max - new_max)
            run_sum = alpha * run_sum + blk_sum
            acc = alpha * acc + blk_acc
        run_max = new_max
        if j == n_kb - 1:
            outs.append((acc / run_sum).T)
        st = st_next
    o_ref[0] = jnp.concatenate(outs, axis=-1).astype(o_ref.dtype)


def _attention(qt, k, vt, *, tq=512, kb=512):
    b, _, l = qt.shape
    assert l % tq == 0 and l % kb == 0
    gw = ATTN_GROUP * ATTN_HEAD_DIM
    return pl.pallas_call(
        functools.partial(_attn_body, kb=kb),
        grid=(b, ATTN_KV_HEADS, l // tq),
        in_specs=[
            pl.BlockSpec((1, gw, tq), lambda bi, g, i: (bi, g, i)),
            pl.BlockSpec((1, 1, l, ATTN_HEAD_DIM), lambda bi, g, i: (bi, g, 0, 0)),
            pl.BlockSpec((1, ATTN_HEAD_DIM, l), lambda bi, g, i: (bi, g, 0)),
        ],
        out_specs=pl.BlockSpec((1, tq, gw), lambda bi, g, i: (bi, i, g)),
        out_shape=jax.ShapeDtypeStruct((b, l, ATTN_WIDTH), BF16),
        compiler_params=_cparams(("parallel", "parallel", "parallel"), 48),
        name="attention",
    )(qt, k, vt)


def _gla_body(qf_ref, kf_ref, vf_ref, zf_ref, qb_ref, kb_ref, vb_ref, zb_ref,
              waf_ref, wab_ref, baf_ref, bab_ref, cumf_ref, cumb_ref,
              of_ref, ob_ref, st_ref):
    @pl.when(pl.program_id(1) == 0)
    def _():
        st_ref[...] = jnp.zeros_like(st_ref)

    tb = qf_ref.shape[1]
    n_chunks = tb // GLA_CHUNK
    both = (0, 1)
    heads = [slice(h * GLA_HEAD_DIM, (h + 1) * GLA_HEAD_DIM) for h in range(GLA_HEADS)]
    chunks = [slice(ci * GLA_CHUNK, (ci + 1) * GLA_CHUNK) for ci in range(n_chunks)]
    q_refs, k_refs, v_refs, z_refs = (qf_ref, qb_ref), (kf_ref, kb_ref), (vf_ref, vb_ref), (zf_ref, zb_ref)
    wa_refs, ba_refs, cum_refs, o_refs = (waf_ref, wab_ref), (baf_ref, bab_ref), (cumf_ref, cumb_ref), (of_ref, ob_ref)

    logits = [_dot(z_refs[d][0].astype(BF16), wa_refs[d][...]) + ba_refs[d][...] for d in both]
    log_a = [(jnp.minimum(x, 0.0) - jnp.log1p(jnp.exp(-jnp.abs(x)))) * (1.0 / GLA_TAU) for x in logits]
    hi = [x.astype(BF16) for x in log_a]
    lo = [(x - h.astype(F32)).astype(BF16) for x, h in zip(log_a, hi)]
    bcum = [_dot(cum_refs[d][...], hi[d]) + _dot(cum_refs[d][...], lo[d]) for d in both]
    q_dec, k_dec, k_end, decay, vb, mask = [], [], [], [], [], []
    r = lax.broadcasted_iota(jnp.int32, (tb, tb), 0)
    c = lax.broadcasted_iota(jnp.int32, (tb, tb), 1)
    same = (r // GLA_CHUNK) == (c // GLA_CHUNK)
    for d in both:
        edge = 0 if d else GLA_CHUNK - 1
        btot = jnp.concatenate(
            [jnp.broadcast_to(bcum[d][rows.start + edge:rows.start + edge + 1], (GLA_CHUNK, bcum[d].shape[1]))
             for rows in chunks], axis=0)
        k = k_refs[d][0]
        q_dec.append(((q_refs[d][0] * GLA_HEAD_DIM ** -0.5) * jnp.exp(bcum[d])).astype(BF16))
        k_dec.append((k * jnp.exp(-bcum[d])).astype(BF16))
        k_end.append((k * jnp.exp(btot - bcum[d])).astype(BF16))
        decay.append(jnp.exp(btot))
        vb.append(v_refs[d][0].astype(BF16))
        mask.append(same & ((c >= r) if d else (c <= r)))

    scores = [[jnp.where(mask[d], _dot_nt(q_dec[d][:, hs], k_dec[d][:, hs]), 0.0).astype(BF16) for hs in heads]
              for d in both]
    o_intra = [[_dot(scores[d][h], vb[d][:, hs]) for h, hs in enumerate(heads)] for d in both]
    kv = [[[_dot_tn(vb[d][rows, hs], k_end[d][rows, hs]) for rows in chunks] for hs in heads] for d in both]

    enter = [[[None] * n_chunks for _ in heads] for _ in both]
    for d in both:
        order = range(n_chunks - 1, -1, -1) if d else range(n_chunks)
        for h, hs in enumerate(heads):
            st = st_ref[d, h]
            for ci in order:
                enter[d][h][ci] = st.astype(BF16)
                st = decay[d][chunks[ci].start:chunks[ci].start + 1, hs] * st + kv[d][h][ci]
            st_ref[d, h] = st

    o_inter = [[[_dot_nt(q_dec[d][rows, hs], enter[d][h][ci]) for ci, rows in enumerate(chunks)]
                for h, hs in enumerate(heads)] for d in both]
    for d in both:
        o_refs[d][0] = jnp.concatenate(
            [o_intra[d][h] + jnp.concatenate(o_inter[d][h], axis=0) for h in range(GLA_HEADS)], axis=-1)


def _gla(proj, w_alpha, b_alpha, *, tb=256):
    b, l, _ = proj.shape
    nblk = l // tb
    hd = GLA_HEAD_DIM
    r = GLA_LOWRANK
    waf = jnp.zeros((LANE, GLA_WIDTH), F32).at[:r].set(w_alpha[0]).astype(BF16)
    wab = jnp.zeros((LANE, GLA_WIDTH), F32).at[r:2 * r].set(w_alpha[1]).astype(BF16)
    ba = b_alpha.reshape(2, 1, GLA_WIDTH)
    idx = np.arange(tb)
    same = (idx[:, None] // GLA_CHUNK) == (idx[None, :] // GLA_CHUNK)
    cum_f = jnp.asarray(same & (idx[None, :] <= idx[:, None]), BF16)
    cum_b = jnp.asarray(same & (idx[None, :] >= idx[:, None]), BF16)

    def col(base, width, rev):
        if rev:
            return pl.BlockSpec((1, tb, width), lambda bi, i: (bi, nblk - 1 - i, base // width))
        return pl.BlockSpec((1, tb, width), lambda bi, i: (bi, i, base // width))

    w = GLA_WIDTH
    in_specs = [col(COL_GQ, w, False), col(COL_GK, w, False), col(COL_GV, w, False), col(COL_Z, LANE, False),
                col(COL_GQ, w, True), col(COL_GK, w, True), col(COL_GV, w, True), col(COL_Z, LANE, True),
                _resident((LANE, w)), _resident((LANE, w)), _resident((1, w)), _resident((1, w)),
                _resident((tb, tb)), _resident((tb, tb))]
    return pl.pallas_call(
        _gla_body,
        grid=(b, nblk),
        in_specs=in_specs,
        out_specs=[col(0, w, False), col(0, w, True)],
        out_shape=[jax.ShapeDtypeStruct((b, l, GLA_WIDTH), F32)] * 2,
        scratch_shapes=[pltpu.VMEM((2, GLA_HEADS, hd, hd), F32)],
        compiler_params=_cparams(("parallel", "arbitrary"), 32),
        name="gla",
    )(*([proj] * 8), waf, wab, ba[0], ba[1], cum_f, cum_b)


def _s5_body(u_ref, tz_ref, win_ref, wout_ref, a_ref, d_ref, o_ref, z_scr, x_scr):
    l = u_ref.shape[1]
    nc = l // S5_CHUNK
    nst = S5_LANE_GROUPS * S5_STATE
    uk = jnp.concatenate([u_ref[0, pl.ds(t, nc, stride=S5_CHUNK), :] for t in range(S5_CHUNK)], axis=-1)
    ukb = uk.astype(BF16)
    z_scr[...] = _dot(ukb, win_ref[0])

    a = a_ref[0]
    afr, afi, abr, abi = (a[:, i * nst:(i + 1) * nst] for i in range(4))
    ntile = nc // 8

    def tile_step(i, carry):
        sfr, sfi, sbr, sbi = carry
        rf = pl.multiple_of(i * 8, 8)
        rb = pl.multiple_of((ntile - 1 - i) * 8, 8)
        zf = z_scr[pl.ds(rf, 8), 0:2 * nst]
        zb = z_scr[pl.ds(rb, 8), 2 * nst:4 * nst]
        xfr, xfi, xbr, xbi = [], [], [None] * 8, [None] * 8
        for r in range(8):
            xfr.append(sfr)
            xfi.append(sfi)
            sfr, sfi = (afr * sfr - afi * sfi + zf[r:r + 1, :nst],
                        afr * sfi + afi * sfr + zf[r:r + 1, nst:])
            q = 7 - r
            xbr[q] = sbr
            xbi[q] = sbi
            sbr, sbi = (abr * sbr - abi * sbi + zb[q:q + 1, :nst],
                        abr * sbi + abi * sbr + zb[q:q + 1, nst:])
        x_scr[pl.ds(rf, 8), 0:nst] = jnp.concatenate(xfr, axis=0)
        x_scr[pl.ds(rf, 8), nst:2 * nst] = jnp.concatenate(xfi, axis=0)
        x_scr[pl.ds(rb, 8), 2 * nst:3 * nst] = jnp.concatenate(xbr, axis=0)
        x_scr[pl.ds(rb, 8), 3 * nst:4 * nst] = jnp.concatenate(xbi, axis=0)
        return sfr, sfi, sbr, sbi

    zero = jnp.zeros((1, nst), F32)
    lax.fori_loop(0, ntile, tile_step, (zero, zero, zero, zero))

    y = _dot(ukb, tz_ref[0]) + _dot(x_scr[...].astype(BF16), wout_ref[0]) + uk * d_ref[0]
    for t in range(S5_CHUNK):
        o_ref[0, pl.ds(t, nc, stride=S5_CHUNK), :] = y[:, t * LANE:(t + 1) * LANE]


def _s5(proj, mats, d_skip):
    b, l, _ = proj.shape
    tz, win, wout, a = mats
    nk = S5_WIDTH // LANE
    nc = l // S5_CHUNK
    feat = S5_CHUNK * LANE
    nst4 = 4 * S5_LANE_GROUPS * S5_STATE
    d_tiled = jnp.tile(d_skip.reshape(nk, 1, LANE), (1, 1, S5_CHUNK))
    return pl.pallas_call(
        _s5_body,
        grid=(nk, b),
        in_specs=[
            pl.BlockSpec((1, l, LANE), lambda k, bi: (bi, 0, COL_S5 // LANE + k)),
            pl.BlockSpec((1, feat, feat), lambda k, bi: (k, 0, 0)),
            pl.BlockSpec((1, feat, nst4), lambda k, bi: (k, 0, 0)),
            pl.BlockSpec((1, nst4, feat), lambda k, bi: (k, 0, 0)),
            pl.BlockSpec((1, 1, nst4), lambda k, bi: (k, 0, 0)),
            pl.BlockSpec((1, 1, feat), lambda k, bi: (k, 0, 0)),
        ],
        out_specs=pl.BlockSpec((1, l, LANE), lambda k, bi: (bi, 0, k)),
        out_shape=jax.ShapeDtypeStruct((b, l, S5_WIDTH), F32),
        scratch_shapes=[pltpu.VMEM((nc, nst4), F32), pltpu.VMEM((nc, nst4), F32)],
        compiler_params=_cparams(("arbitrary", "arbitrary"), 48),
        name="s5",
    )(proj, tz, win, wout, a, d_tiled)


def _cmul(ar, ai, br, bi):
    return ar * br - ai * bi, ar * bi + ai * br


def _s5_matrices(lam_re, lam_im, log_dt, b_re, b_im, c_re, c_im):
    g, p, h, tc, lg = S5_GROUPS, S5_STATE, S5_GROUP_CH, S5_CHUNK, S5_LANE_GROUPS
    nk = g // lg
    dt = jnp.exp(log_dt)[..., None]
    mag = jnp.exp(lam_re * dt)
    lbr, lbi = mag * jnp.cos(lam_im * dt), mag * jnp.sin(lam_im * dt)
    den = lam_re * lam_re + lam_im * lam_im
    nr, ni = lbr - 1.0, lbi
    fr, fi = (nr * lam_re + ni * lam_im) / den, (ni * lam_re - nr * lam_im) / den
    bbr, bbi = _cmul(fr[..., None], fi[..., None], b_re, b_im)
    pwr, pwi = [jnp.ones_like(lbr)], [jnp.zeros_like(lbi)]
    for _ in range(tc):
        nr_, ni_ = _cmul(pwr[-1], pwi[-1], lbr, lbi)
        pwr.append(nr_)
        pwi.append(ni_)
    pwr, pwi = jnp.stack(pwr, 1), jnp.stack(pwi, 1)

    hi = lax.Precision.HIGHEST
    mr, mi = _cmul(c_re[:, None], c_im[:, None], pwr[:, :, :, None, :], pwi[:, :, :, None, :])
    kern = (jnp.einsum('djghp,dgpx->djghx', mr, bbr, precision=hi)
            - jnp.einsum('djghp,dgpx->djghx', mi, bbi, precision=hi))
    eye = jnp.eye(lg, dtype=F32)
    comb = jnp.concatenate([kern[1][tc - 1:0:-1], (kern[0][0] + kern[1][0])[None], kern[0][1:tc]])
    comb = comb.transpose(0, 1, 3, 2).reshape(2 * tc - 1, nk, lg, h, h)
    blk = (comb[:, :, :, :, None, :] * eye[None, None, :, None, :, None]).reshape(2 * tc - 1, nk, lg * h, lg * h)
    lag = np.arange(tc)[None, :] - np.arange(tc)[:, None] + tc - 1
    tz = blk.astype(BF16)[lag]
    tz = tz.transpose(2, 0, 3, 1, 4).reshape(nk, tc * lg * h, tc * lg * h)

    sel_f = np.arange(tc - 1, -1, -1)
    sel_b = np.arange(tc)
    parts = []
    for d, sel in ((0, sel_f), (1, sel_b)):
        r_, i_ = _cmul(pwr[d][sel][..., None], pwi[d][sel][..., None], bbr[d][None], bbi[d][None])
        parts += [r_, i_]
    w = jnp.stack(parts)
    w = w.reshape(4, tc, nk, lg, p, h).transpose(0, 2, 1, 5, 3, 4).reshape(4, nk, tc, 1, h, lg * p)
    col_group = jnp.asarray((np.arange(lg * p) // p)[None, :] == np.arange(lg)[:, None])
    win = jnp.where(col_group[None, None, None, :, None, :], w.astype(BF16), 0)
    win = jnp.concatenate([win[i].reshape(nk, tc * lg * h, lg * p) for i in range(4)], axis=-1)

    sel_f = np.arange(1, tc + 1)
    sel_b = np.arange(tc, 0, -1)
    parts = []
    for d, sel in ((0, sel_f), (1, sel_b)):
        r_, i_ = _cmul(c_re[d][None], c_im[d][None], pwr[d][sel][:, :, None, :], pwi[d][sel][:, :, None, :])
        parts += [r_, -i_]
    wo = jnp.stack(parts)
    wo = wo.reshape(4, tc, nk, lg, h, p).transpose(2, 0, 5, 1, 3, 4).reshape(nk, 4, 1, p, tc * lg * h)
    out_group = jnp.asarray(((np.arange(tc * lg * h) // h) % lg)[None, :] == np.arange(lg)[:, None])
    wout = jnp.where(out_group[None, None, :, None, :], wo.astype(BF16), 0)
    wout = wout.reshape(nk, 4 * lg * p, tc * lg * h)

    a = jnp.stack([pwr[0, tc], pwi[0, tc], pwr[1, tc], pwi[1, tc]])
    a = a.reshape(4, nk, lg * p).transpose(1, 0, 2).reshape(nk, 1, 4 * lg * p)
    return tz, win, wout, a


def _merge_body(x_ref, ys_ref, of_ref, ob_ref, gate_ref, ya_ref, mg_ref, wm_ref, bm_ref, wglu_ref,
                gg_ref, wbs_ref, wbg_ref, wba_ref, wo_ref, o_ref):
    x = x_ref[...]
    d = x.shape[1]
    h = _rms(x, mg_ref[...]).astype(BF16)

    y = jax.nn.gelu(ys_ref[...])
    y_s5 = y * jax.nn.sigmoid(_dot(y.astype(BF16), wglu_ref[...]))

    o = of_ref[...] + ob_ref[...]
    heads = []
    for hh in range(GLA_HEADS):
        oh = o[:, hh * GLA_HEAD_DIM:(hh + 1) * GLA_HEAD_DIM]
        heads.append(oh * lax.rsqrt(jnp.mean(oh * oh, axis=-1, keepdims=True) + NORM_EPS))
    gate = gate_ref[...]
    y_gla = jnp.concatenate(heads, axis=-1) * gg_ref[...] * (gate * jax.nn.sigmoid(gate))

    merged = None
    for i, (val, w_ref) in enumerate(((y_s5.astype(BF16), wbs_ref), (y_gla.astype(BF16), wbg_ref),
                                      (ya_ref[...], wba_ref))):
        g = jax.nn.sigmoid(_dot(h, wm_ref[:, i * d:(i + 1) * d]) + bm_ref[:, i * d:(i + 1) * d])
        term = g * _dot(val, w_ref[...])
        merged = term if merged is None else merged + term
    o_ref[...] = x + _dot(merged.astype(BF16), wo_ref[...])


def _merge(x2d, ys5, o_f, o_b, proj2d, y_attn, mix_gain, w_merge, b_merge, w_glu, gla_gain,
           wb_s5, wb_gla, wb_attn, w_out, *, tm=512):
    t, d = x2d.shape
    row = lambda w: pl.BlockSpec((tm, w), lambda i: (i, 0))
    return pl.pallas_call(
        _merge_body,
        grid=(t // tm,),
        in_specs=[
            row(d), row(S5_WIDTH), row(GLA_WIDTH), row(GLA_WIDTH),
            pl.BlockSpec((tm, GLA_WIDTH), lambda i: (i, COL_GG // GLA_WIDTH)),
            row(ATTN_WIDTH),
            _resident((1, d)), _resident(w_merge.shape), _resident((1, w_merge.shape[1])),
            _resident(w_glu.shape), _resident((1, GLA_WIDTH)),
            _resident(wb_s5.shape), _resident(wb_gla.shape), _resident(wb_attn.shape), _resident(w_out.shape),
        ],
        out_specs=row(d),
        out_shape=jax.ShapeDtypeStruct((t, d), F32),
        compiler_params=_cparams(("parallel",), 48),
        name="merge",
    )(x2d, ys5, o_f, o_b, proj2d, y_attn, mix_gain.reshape(1, d), w_merge, b_merge.reshape(1, -1), w_glu,
      jnp.tile(gla_gain, GLA_HEADS).reshape(1, GLA_WIDTH), wb_s5, wb_gla, wb_attn, w_out)


def _reorder_w_in(w_in):
    z0 = COL_AQ
    z1 = z0 + 2 * GLA_LOWRANK
    d = w_in.shape[0]
    pad = jnp.zeros((d, PROJ_WIDTH - w_in.shape[1]), w_in.dtype)
    return jnp.concatenate([w_in[:, :z0], w_in[:, z1:], w_in[:, z0:z1], pad], axis=1)


def kernel(x, ffn1_norm, ffn1_w_gate, ffn1_w_up, ffn1_w_down, mix_norm, w_in, s5_lambda_re, s5_lambda_im, s5_log_dt, s5_b_re, s5_b_im, s5_c_re, s5_c_im, s5_d, s5_w_glu, gla_w_alpha, gla_b_alpha, gla_norm, attn_q_norm, attn_k_norm, w_branch_s5, w_branch_gla, w_branch_attn, w_merge_gate, b_merge_gate, w_out, ffn2_norm, ffn2_w_gate, ffn2_w_up, ffn2_w_down, final_norm):
    bsz, seq_len, d_model = x.shape
    depth = w_in.shape[0]
    tabs = _rope_tables(seq_len)
    bf = lambda w: w.astype(BF16)
    x2d = x.reshape(bsz * seq_len, d_model)
    for i in range(depth):
        x2d = _ffn(x2d, ffn1_norm[i], bf(ffn1_w_gate[i]), bf(ffn1_w_up[i]), bf(ffn1_w_down[i]))

        proj2d = _inproj(x2d, mix_norm[i], bf(_reorder_w_in(w_in[i])))
        proj = proj2d.reshape(bsz, seq_len, PROJ_WIDTH)
        mats = _s5_matrices(s5_lambda_re[i], s5_lambda_im[i], s5_log_dt[i], s5_b_re[i], s5_b_im[i],
                            s5_c_re[i], s5_c_im[i])
        ys5 = _s5(proj, mats, s5_d[i])
        o_f, o_b = _gla(proj, gla_w_alpha[i], gla_b_alpha[i])
        y_attn = _attention(*_attn_prep(proj, tabs, attn_q_norm[i], attn_k_norm[i]))

        flat = lambda a: a.reshape(bsz * seq_len, a.shape[-1])
        x2d = _merge(x2d, flat(ys5), flat(o_f), flat(o_b), proj2d, flat(y_attn), mix_norm[i],
                     bf(w_merge_gate[i]), b_merge_gate[i], bf(s5_w_glu[i]), gla_norm[i],
                     bf(w_branch_s5[i]), bf(w_branch_gla[i]), bf(w_branch_attn[i]), bf(w_out[i]))

        last = i == depth - 1
        x2d = _ffn(x2d, ffn2_norm[i], bf(ffn2_w_gate[i]), bf(ffn2_w_up[i]), bf(ffn2_w_down[i]),
                   final_norm if last else None)
    return x2d.reshape(bsz, seq_len, d_model)
```

```python
import functools
import math

import jax
import jax.numpy as jnp
import numpy as np
from jax import lax
from jax.experimental import pallas as pl
from jax.experimental.pallas import tpu as pltpu

F32 = jnp.float32
BF16 = jnp.bfloat16

NORM_EPS = 1e-6
S5_GROUPS = 32
S5_GROUP_CH = 16
S5_STATE = 64
S5_WIDTH = S5_GROUPS * S5_GROUP_CH
S5_CHUNK = 8
S5_LANE_GROUPS = 8
GLA_HEADS = 4
GLA_HEAD_DIM = 128
GLA_WIDTH = GLA_HEADS * GLA_HEAD_DIM
GLA_LOWRANK = 16
GLA_TAU = 16.0
GLA_CHUNK = 64
ATTN_Q_HEADS = 8
ATTN_KV_HEADS = 2
ATTN_HEAD_DIM = 64
ATTN_GROUP = ATTN_Q_HEADS // ATTN_KV_HEADS
ATTN_WIDTH = ATTN_Q_HEADS * ATTN_HEAD_DIM
ATTN_KV_WIDTH = ATTN_KV_HEADS * ATTN_HEAD_DIM
GRID_W = 64
ROPE_BASE = 10000.0
ROPE_PAIR = ATTN_HEAD_DIM // 4

LANE = 128
V7X_VMEM_BYTES = 64 * 1024 * 1024

COL_S5 = 0
COL_GQ = 512
COL_GK = 1024
COL_GV = 1536
COL_GG = 2048
COL_AQ = 2560
COL_AKV = 3072
COL_Z = 3328
PROJ_WIDTH = 3456


def _cparams(semantics, vmem_mib):
    return pltpu.CompilerParams(
        dimension_semantics=semantics,
        vmem_limit_bytes=min(vmem_mib * 1024 * 1024, V7X_VMEM_BYTES - 4 * 1024 * 1024),
    )


def _resident(shape):
    nd = len(shape)
    return pl.BlockSpec(shape, lambda *_: (0,) * nd, pipeline_mode=pl.Buffered(1))


def _layer(shape, layer):
    nd = len(shape)
    return pl.BlockSpec((None,) + tuple(shape), lambda *_: (layer,) + (0,) * nd, pipeline_mode=pl.Buffered(1))


def _rms(x, gain):
    ms = jnp.mean(x * x, axis=-1, keepdims=True)
    return x * lax.rsqrt(ms + NORM_EPS) * gain


def _dot(a, b):
    return jnp.dot(a, b, preferred_element_type=F32)


def _dot_nt(a, b):
    return lax.dot_general(a, b, (((1,), (1,)), ((), ())), preferred_element_type=F32)


def _dot_tn(a, b):
    return lax.dot_general(a, b, (((0,), (0,)), ((), ())), preferred_element_type=F32)


def _ffn_body(x_ref, gain_ref, wg_ref, wu_ref, wd_ref, *rest, chunk, final):
    if final:
        fg_ref, o_ref, a_ref = rest
    else:
        o_ref, a_ref = rest
    x = x_ref[...]
    h = _rms(x, gain_ref[...]).astype(BF16)
    d_ff = wg_ref.shape[1]
    for c0 in range(0, d_ff, chunk):
        g = _dot(h, wg_ref[:, c0:c0 + chunk])
        u = _dot(h, wu_ref[:, c0:c0 + chunk])
        a_ref[:, c0:c0 + chunk] = (g * jax.nn.sigmoid(g) * u).astype(BF16)
    out = x + 0.5 * _dot(a_ref[...], wd_ref[...])
    if final:
        out = _rms(out, fg_ref[...])
    o_ref[...] = out


def _ffn(x2d, gain, wg, wu, wd, layer, final_gain=None, *, tm=512, chunk=256):
    t, d = x2d.shape
    f = wg.shape[2]
    final = final_gain is not None
    row = pl.BlockSpec((tm, d), lambda i: (i, 0))
    in_specs = [row, _layer((1, d), layer), _layer((d, f), layer), _layer((d, f), layer), _layer((f, d), layer)]
    args = [x2d, gain, wg, wu, wd]
    if final:
        in_specs.append(_resident((1, d)))
        args.append(final_gain.reshape(1, d))
    return pl.pallas_call(
        functools.partial(_ffn_body, chunk=chunk, final=final),
        grid=(t // tm,),
        in_specs=in_specs,
        out_specs=row,
        out_shape=jax.ShapeDtypeStruct((t, d), F32),
        scratch_shapes=[pltpu.VMEM((tm, f), BF16)],
        compiler_params=_cparams(("parallel",), 48),
        name="ffn",
    )(*args)


def _inproj_body(x_ref, gain_ref, w_ref, o_ref, *, chunk):
    h = _rms(x_ref[...], gain_ref[...]).astype(BF16)
    n = w_ref.shape[1]
    for c0 in range(0, n, chunk):
        c1 = min(c0 + chunk, n)
        o_ref[:, c0:c1] = _dot(h, w_ref[:, c0:c1])


def _inproj(x2d, gain, w, layer, *, tm=512, chunk=512):
    t, d = x2d.shape
    n = w.shape[2]
    return pl.pallas_call(
        functools.partial(_inproj_body, chunk=chunk),
        grid=(t // tm,),
        in_specs=[pl.BlockSpec((tm, d), lambda i: (i, 0)), _layer((1, d), layer), _layer((d, n), layer)],
        out_specs=pl.BlockSpec((tm, n), lambda i: (i, 0)),
        out_shape=jax.ShapeDtypeStruct((t, n), F32),
        compiler_params=_cparams(("parallel",), 40),
        name="inproj",
    )(x2d, gain, w)


def _head_rms(x, ones_ref, gain):
    x2 = x * x
    hi = x2.astype(BF16)
    lo = (x2 - hi.astype(F32)).astype(BF16)
    ss = _dot(hi, ones_ref[...]) + _dot(lo, ones_ref[...])
    return x * lax.rsqrt(ss * (1.0 / ATTN_HEAD_DIM) + NORM_EPS) * gain


def _rope(x, cos, sin_signed):
    n = x.shape[-1]
    lane = lax.broadcasted_iota(jnp.int32, x.shape, 1)
    first = (lane & ROPE_PAIR) == 0
    partner = jnp.where(first, pltpu.roll(x, n - ROPE_PAIR, 1), pltpu.roll(x, ROPE_PAIR, 1))
    return x * cos + partner * sin_signed


def _attn_prep_body(q_ref, kv_ref, cq_ref, sq_ref, ck_ref, sk_ref, gq_ref, gk_ref, oq_ref, ok_ref,
                    qt_ref, k_ref, vt_ref):
    q = _rope(_head_rms(q_ref[0], oq_ref, gq_ref[...]), cq_ref[...], sq_ref[...])
    qt_ref[0] = q.T.astype(BF16)
    kv = kv_ref[0]
    k = _rope(_head_rms(kv[:, :ATTN_KV_WIDTH], ok_ref, gk_ref[...]), ck_ref[...], sk_ref[...])
    for g in range(ATTN_KV_HEADS):
        k_ref[0, g] = k[:, g * ATTN_HEAD_DIM:(g + 1) * ATTN_HEAD_DIM].astype(BF16)
    vt_ref[0] = kv[:, ATTN_KV_WIDTH:].T.astype(BF16)


def _attn_prep(proj, tabs, gq, gk, layer, *, tm=512):
    b, l, _ = proj.shape
    cq, sq, ck, sk = tabs
    ones_q = jnp.asarray(np.kron(np.eye(ATTN_Q_HEADS), np.ones((ATTN_HEAD_DIM, ATTN_HEAD_DIM))), BF16)
    ones_k = jnp.asarray(np.kron(np.eye(ATTN_KV_HEADS), np.ones((ATTN_HEAD_DIM, ATTN_HEAD_DIM))), BF16)
    kvw = 2 * ATTN_KV_WIDTH
    tab_q = pl.BlockSpec((tm, ATTN_WIDTH), lambda bi, i: (i, 0))
    tab_k = pl.BlockSpec((tm, ATTN_KV_WIDTH), lambda bi, i: (i, 0))
    return pl.pallas_call(
        _attn_prep_body,
        grid=(b, l // tm),
        in_specs=[
            pl.BlockSpec((1, tm, ATTN_WIDTH), lambda bi, i: (bi, i, COL_AQ // ATTN_WIDTH)),
            pl.BlockSpec((1, tm, kvw), lambda bi, i: (bi, i, COL_AKV // kvw)),
            tab_q, tab_q, tab_k, tab_k,
            _layer((1, ATTN_WIDTH), layer), _layer((1, ATTN_KV_WIDTH), layer),
            _resident((ATTN_WIDTH, ATTN_WIDTH)), _resident((ATTN_KV_WIDTH, ATTN_KV_WIDTH)),
        ],
        out_specs=[
            pl.BlockSpec((1, ATTN_WIDTH, tm), lambda bi, i: (bi, 0, i)),
            pl.BlockSpec((1, ATTN_KV_HEADS, tm, ATTN_HEAD_DIM), lambda bi, i: (bi, 0, i, 0)),
            pl.BlockSpec((1, ATTN_KV_WIDTH, tm), lambda bi, i: (bi, 0, i)),
        ],
        out_shape=[
            jax.ShapeDtypeStruct((b, ATTN_WIDTH, l), BF16),
            jax.ShapeDtypeStruct((b, ATTN_KV_HEADS, l, ATTN_HEAD_DIM), BF16),
            jax.ShapeDtypeStruct((b, ATTN_KV_WIDTH, l), BF16),
        ],
        compiler_params=_cparams(("parallel", "parallel"), 32),
        name="attn_prep",
    )(proj, proj, cq, sq, ck, sk, gq, gk, ones_q, ones_k)


def _rope_tables(l):
    t = jnp.arange(l, dtype=jnp.int32)
    rows = (t // GRID_W).astype(F32)
    cols = (t % GRID_W).astype(F32)
    half = ATTN_HEAD_DIM // 2
    inv_freq = ROPE_BASE ** (-jnp.arange(half // 2, dtype=F32) * 2.0 / half)
    d = np.arange(ATTN_HEAD_DIM)
    use_rows = jnp.asarray((d // half) == 0)
    freq = inv_freq[jnp.asarray(d % (half // 2))]
    second = jnp.asarray((d % half) >= half // 2)
    ang = jnp.where(use_rows[None, :], rows[:, None], cols[:, None]) * freq[None, :]
    cos = jnp.cos(ang)
    sin = jnp.sin(ang)
    sin = jnp.where(second[None, :], sin, -sin)
    scale = ATTN_HEAD_DIM ** -0.5 * math.log2(math.e)
    return (jnp.tile(cos, (1, ATTN_Q_HEADS)) * scale, jnp.tile(sin, (1, ATTN_Q_HEADS)) * scale,
            jnp.tile(cos, (1, ATTN_KV_HEADS)), jnp.tile(sin, (1, ATTN_KV_HEADS)))


def _attn_body(qt_ref, k_ref, vt_ref, o_ref, *, kb):
    n_kb = k_ref.shape[2] // kb
    items = [(h, j) for h in range(ATTN_GROUP) for j in range(n_kb)]

    def scores(h, j):
        qt = qt_ref[0, h * ATTN_HEAD_DIM:(h + 1) * ATTN_HEAD_DIM, :]
        return _dot(k_ref[0, 0, j * kb:(j + 1) * kb, :], qt)

    outs = []
    st = scores(*items[0])
    run_max = run_sum = acc = None
    for idx, (h, j) in enumerate(items):
        st_next = scores(*items[idx + 1]) if idx + 1 < len(items) else None
        blk_max = jnp.max(st, axis=0, keepdims=True)
        new_max = blk_max if j == 0 else jnp.maximum(run_max, blk_max)
        p = jnp.exp2(st - new_max)
        blk_sum = jnp.sum(p, axis=0, keepdims=True)
        blk_acc = _dot(vt_ref[0, :, j * kb:(j + 1) * kb], p.astype(BF16))
        if j == 0:
            run_sum, acc = blk_sum, blk_acc
        else:
            alpha = jnp.exp2(run_max - new_max)
            run_sum = alpha * run_sum + blk_sum
            acc = alpha * acc + blk_acc
        run_max = new_max
        if j == n_kb - 1:
            outs.append((acc / run_sum).T)
        st = st_next
    o_ref[0] = jnp.concatenate(outs, axis=-1).astype(o_ref.dtype)


def _attention(qt, k, vt, *, tq=512, kb=512):
    b, _, l = qt.shape
    assert l % tq == 0 and l % kb == 0
    gw = ATTN_GROUP * ATTN_HEAD_DIM
    return pl.pallas_call(
        functools.partial(_attn_body, kb=kb),
        grid=(b, ATTN_KV_HEADS, l // tq),
        in_specs=[
            pl.BlockSpec((1, gw, tq), lambda bi, g, i: (bi, g, i)),
            pl.BlockSpec((1, 1, l, ATTN_HEAD_DIM), lambda bi, g, i: (bi, g, 0, 0)),
            pl.BlockSpec((1, ATTN_HEAD_DIM, l), lambda bi, g, i: (bi, g, 0)),
        ],
        out_specs=pl.BlockSpec((1, tq, gw), lambda bi, g, i: (bi, i, g)),
        out_shape=jax.ShapeDtypeStruct((b, l, ATTN_WIDTH), BF16),
        compiler_params=_cparams(("parallel", "parallel", "parallel"), 48),
        name="attention",
    )(qt, k, vt)


def _gla_body(qf_ref, kf_ref, vf_ref, zf_ref, qb_ref, kb_ref, vb_ref, zb_ref,
              waf_ref, wab_ref, baf_ref, bab_ref, cumf_ref, cumb_ref,
              of_ref, ob_ref, st_ref):
    @pl.when(pl.program_id(1) == 0)
    def _():
        st_ref[...] = jnp.zeros_like(st_ref)

    tb = qf_ref.shape[1]
    n_chunks = tb // GLA_CHUNK
    both = (0, 1)
    heads = [slice(h * GLA_HEAD_DIM, (h + 1) * GLA_HEAD_DIM) for h in range(GLA_HEADS)]
    chunks = [slice(ci * GLA_CHUNK, (ci + 1) * GLA_CHUNK) for ci in range(n_chunks)]
    q_refs, k_refs, v_refs, z_refs = (qf_ref, qb_ref), (kf_ref, kb_ref), (vf_ref, vb_ref), (zf_ref, zb_ref)
    wa_refs, ba_refs, cum_refs, o_refs = (waf_ref, wab_ref), (baf_ref, bab_ref), (cumf_ref, cumb_ref), (of_ref, ob_ref)

    logits = [_dot(z_refs[d][0].astype(BF16), wa_refs[d][...]) + ba_refs[d][...] for d in both]
    log_a = [(jnp.minimum(x, 0.0) - jnp.log1p(jnp.exp(-jnp.abs(x)))) * (1.0 / GLA_TAU) for x in logits]
    hi = [x.astype(BF16) for x in log_a]
    lo = [(x - h.astype(F32)).astype(BF16) for x, h in zip(log_a, hi)]
    bcum = [_dot(cum_refs[d][...], hi[d]) + _dot(cum_refs[d][...], lo[d]) for d in both]
    q_dec, k_dec, k_end, decay, vb, mask = [], [], [], [], [], []
    r = lax.broadcasted_iota(jnp.int32, (tb, tb), 0)
    c = lax.broadcasted_iota(jnp.int32, (tb, tb), 1)
    same = (r // GLA_CHUNK) == (c // GLA_CHUNK)
    for d in both:
        edge = 0 if d else GLA_CHUNK - 1
        btot = jnp.concatenate(
            [jnp.broadcast_to(bcum[d][rows.start + edge:rows.start + edge + 1], (GLA_CHUNK, bcum[d].shape[1]))
             for rows in chunks], axis=0)
        k = k_refs[d][0]
        q_dec.append(((q_refs[d][0] * GLA_HEAD_DIM ** -0.5) * jnp.exp(bcum[d])).astype(BF16))
        k_dec.append((k * jnp.exp(-bcum[d])).astype(BF16))
        k_end.append((k * jnp.exp(btot - bcum[d])).astype(BF16))
        decay.append(jnp.exp(btot))
        vb.append(v_refs[d][0].astype(BF16))
        mask.append(same & ((c >= r) if d else (c <= r)))

    scores = [[jnp.where(mask[d], _dot_nt(q_dec[d][:, hs], k_dec[d][:, hs]), 0.0).astype(BF16) for hs in heads]
              for d in both]
    o_intra = [[_dot(scores[d][h], vb[d][:, hs]) for h, hs in enumerate(heads)] for d in both]
    kv = [[[_dot_tn(vb[d][rows, hs], k_end[d][rows, hs]) for rows in chunks] for hs in heads] for d in both]

    enter = [[[None] * n_chunks for _ in heads] for _ in both]
    for d in both:
        order = range(n_chunks - 1, -1, -1) if d else range(n_chunks)
        for h, hs in enumerate(heads):
            st = st_ref[d, h]
            for ci in order:
                enter[d][h][ci] = st.astype(BF16)
                st = decay[d][chunks[ci].start:chunks[ci].start + 1, hs] * st + kv[d][h][ci]
            st_ref[d, h] = st

    o_inter = [[[_dot_nt(q_dec[d][rows, hs], enter[d][h][ci]) for ci, rows in enumerate(chunks)]
                for h, hs in enumerate(heads)] for d in both]
    for d in both:
        o_refs[d][0] = jnp.concatenate(
            [o_intra[d][h] + jnp.concatenate(o_inter[d][h], axis=0) for h in range(GLA_HEADS)], axis=-1)


def _gla_gate_params(w_alpha, b_alpha):
    depth = w_alpha.shape[0]
    r = GLA_LOWRANK
    zeros = jnp.zeros((depth, LANE, GLA_WIDTH), F32)
    waf = zeros.at[:, :r].set(w_alpha[:, 0]).astype(BF16)
    wab = zeros.at[:, r:2 * r].set(w_alpha[:, 1]).astype(BF16)
    return waf, wab, b_alpha[:, 0].reshape(depth, 1, GLA_WIDTH), b_alpha[:, 1].reshape(depth, 1, GLA_WIDTH)


def _gla(proj, gate_params, layer, *, tb=256):
    b, l, _ = proj.shape
    nblk = l // tb
    idx = np.arange(tb)
    same = (idx[:, None] // GLA_CHUNK) == (idx[None, :] // GLA_CHUNK)
    cum_f = jnp.asarray(same & (idx[None, :] <= idx[:, None]), BF16)
    cum_b = jnp.asarray(same & (idx[None, :] >= idx[:, None]), BF16)

    def col(base, width, rev):
        if rev:
            return pl.BlockSpec((1, tb, width), lambda bi, i: (bi, nblk - 1 - i, base // width))
        return pl.BlockSpec((1, tb, width), lambda bi, i: (bi, i, base // width))

    w = GLA_WIDTH
    in_specs = [col(COL_GQ, w, False), col(COL_GK, w, False), col(COL_GV, w, False), col(COL_Z, LANE, False),
                col(COL_GQ, w, True), col(COL_GK, w, True), col(COL_GV, w, True), col(COL_Z, LANE, True),
                _layer((LANE, w), layer), _layer((LANE, w), layer), _layer((1, w), layer), _layer((1, w), layer),
                _resident((tb, tb)), _resident((tb, tb))]
    return pl.pallas_call(
        _gla_body,
        grid=(b, nblk),
        in_specs=in_specs,
        out_specs=[col(0, w, False), col(0, w, True)],
        out_shape=[jax.ShapeDtypeStruct((b, l, GLA_WIDTH), F32)] * 2,
        scratch_shapes=[pltpu.VMEM((2, GLA_HEADS, GLA_HEAD_DIM, GLA_HEAD_DIM), F32)],
        compiler_params=_cparams(("parallel", "arbitrary"), 32),
        name="gla",
    )(*([proj] * 8), *gate_params, cum_f, cum_b)


def _s5_body(u_ref, tz_ref, win_ref, wout_ref, a_ref, d_ref, o_ref, z_scr, x_scr):
    l = u_ref.shape[1]
    nc = l // S5_CHUNK
    nst = S5_LANE_GROUPS * S5_STATE
    uk = jnp.concatenate([u_ref[0, pl.ds(t, nc, stride=S5_CHUNK), :] for t in range(S5_CHUNK)], axis=-1)
    ukb = uk.astype(BF16)
    z_scr[...] = _dot(ukb, win_ref[0])
    y_local = _dot(ukb, tz_ref[0]) + uk * d_ref[0]

    a = a_ref[0]
    afr, afi, abr, abi = (a[:, i * nst:(i + 1) * nst] for i in range(4))
    ntile = nc // 8

    def tile_step(i, carry):
        sfr, sfi, sbr, sbi = carry
        rf = pl.multiple_of(i * 8, 8)
        rb = pl.multiple_of((ntile - 1 - i) * 8, 8)
        zf = z_scr[pl.ds(rf, 8), 0:2 * nst]
        zb = z_scr[pl.ds(rb, 8), 2 * nst:4 * nst]
        xfr, xfi, xbr, xbi = [], [], [None] * 8, [None] * 8
        for r in range(8):
            xfr.append(sfr)
            xfi.append(sfi)
            sfr, sfi = (afr * sfr - afi * sfi + zf[r:r + 1, :nst],
                        afr * sfi + afi * sfr + zf[r:r + 1, nst:])
            q = 7 - r
            xbr[q] = sbr
            xbi[q] = sbi
            sbr, sbi = (abr * sbr - abi * sbi + zb[q:q + 1, :nst],
                        abr * sbi + abi * sbr + zb[q:q + 1, nst:])
        x_scr[pl.ds(rf, 8), 0:nst] = jnp.concatenate(xfr, axis=0)
        x_scr[pl.ds(rf, 8), nst:2 * nst] = jnp.concatenate(xfi, axis=0)
        x_scr[pl.ds(rb, 8), 2 * nst:3 * nst] = jnp.concatenate(xbr, axis=0)
        x_scr[pl.ds(rb, 8), 3 * nst:4 * nst] = jnp.concatenate(xbi, axis=0)
        return sfr, sfi, sbr, sbi

    zero = jnp.zeros((1, nst), F32)
    lax.fori_loop(0, ntile, tile_step, (zero, zero, zero, zero), unroll=True)

    y = y_local + _dot(x_scr[...].astype(BF16), wout_ref[0])
    for t in range(S5_CHUNK):
        o_ref[0, pl.ds(t, nc, stride=S5_CHUNK), :] = y[:, t * LANE:(t + 1) * LANE]


def _s5(proj, mats, d_tiled, layer):
    b, l, _ = proj.shape
    tz, win, wout, a = mats
    nk = S5_WIDTH // LANE
    nc = l // S5_CHUNK
    feat = S5_CHUNK * LANE
    nst4 = 4 * S5_LANE_GROUPS * S5_STATE
    return pl.pallas_call(
        _s5_body,
        grid=(nk, b),
        in_specs=[
            pl.BlockSpec((1, l, LANE), lambda k, bi: (bi, 0, COL_S5 // LANE + k)),
            pl.BlockSpec((None, 1, feat, feat), lambda k, bi: (layer, k, 0, 0)),
            pl.BlockSpec((None, 1, feat, nst4), lambda k, bi: (layer, k, 0, 0)),
            pl.BlockSpec((None, 1, nst4, feat), lambda k, bi: (layer, k, 0, 0)),
            pl.BlockSpec((None, 1, 1, nst4), lambda k, bi: (layer, k, 0, 0)),
            pl.BlockSpec((None, 1, 1, feat), lambda k, bi: (layer, k, 0, 0)),
        ],
        out_specs=pl.BlockSpec((1, l, LANE), lambda k, bi: (bi, 0, k)),
        out_shape=jax.ShapeDtypeStruct((b, l, S5_WIDTH), F32),
        scratch_shapes=[pltpu.VMEM((nc, nst4), F32), pltpu.VMEM((nc, nst4), F32)],
        compiler_params=_cparams(("arbitrary", "arbitrary"), 48),
        name="s5",
    )(proj, tz, win, wout, a, d_tiled)


def _cmul(ar, ai, br, bi):
    return ar * br - ai * bi, ar * bi + ai * br


def _s5_matrices(lam_re, lam_im, log_dt, b_re, b_im, c_re, c_im):
    g, p, h, tc, lg = S5_GROUPS, S5_STATE, S5_GROUP_CH, S5_CHUNK, S5_LANE_GROUPS
    nk = g // lg
    dt = jnp.exp(log_dt)[..., None]
    mag = jnp.exp(lam_re * dt)
    lbr, lbi = mag * jnp.cos(lam_im * dt), mag * jnp.sin(lam_im * dt)
    den = lam_re * lam_re + lam_im * lam_im
    nr, ni = lbr - 1.0, lbi
    fr, fi = (nr * lam_re + ni * lam_im) / den, (ni * lam_re - nr * lam_im) / den
    bbr, bbi = _cmul(fr[..., None], fi[..., None], b_re, b_im)
    pwr, pwi = [jnp.ones_like(lbr)], [jnp.zeros_like(lbi)]
    for _ in range(tc):
        nr_, ni_ = _cmul(pwr[-1], pwi[-1], lbr, lbi)
        pwr.append(nr_)
        pwi.append(ni_)
    pwr, pwi = jnp.stack(pwr, 1), jnp.stack(pwi, 1)

    hi = lax.Precision.HIGHEST
    mr, mi = _cmul(c_re[:, None], c_im[:, None], pwr[:, :, :, None, :], pwi[:, :, :, None, :])
    kern = (jnp.einsum('djghp,dgpx->djghx', mr, bbr, precision=hi)
            - jnp.einsum('djghp,dgpx->djghx', mi, bbi, precision=hi))
    eye = jnp.eye(lg, dtype=F32)
    comb = jnp.concatenate([kern[1][tc - 1:0:-1], (kern[0][0] + kern[1][0])[None], kern[0][1:tc]])
    comb = comb.transpose(0, 1, 3, 2).reshape(2 * tc - 1, nk, lg, h, h)
    blk = (comb[:, :, :, :, None, :] * eye[None, None, :, None, :, None]).reshape(2 * tc - 1, nk, lg * h, lg * h)
    lag = np.arange(tc)[None, :] - np.arange(tc)[:, None] + tc - 1
    tz = blk.astype(BF16)[lag]
    tz = tz.transpose(2, 0, 3, 1, 4).reshape(nk, tc * lg * h, tc * lg * h)

    sel_f = np.arange(tc - 1, -1, -1)
    sel_b = np.arange(tc)
    parts = []
    for d, sel in ((0, sel_f), (1, sel_b)):
        r_, i_ = _cmul(pwr[d][sel][..., None], pwi[d][sel][..., None], bbr[d][None], bbi[d][None])
        parts += [r_, i_]
    w = jnp.stack(parts)
    w = w.reshape(4, tc, nk, lg, p, h).transpose(0, 2, 1, 5, 3, 4).reshape(4, nk, tc, 1, h, lg * p)
    col_group = jnp.asarray((np.arange(lg * p) // p)[None, :] == np.arange(lg)[:, None])
    win = jnp.where(col_group[None, None, None, :, None, :], w.astype(BF16), 0)
    win = jnp.concatenate([win[i].reshape(nk, tc * lg * h, lg * p) for i in range(4)], axis=-1)

    sel_f = np.arange(1, tc + 1)
    sel_b = np.arange(tc, 0, -1)
    parts = []
    for d, sel in ((0, sel_f), (1, sel_b)):
        r_, i_ = _cmul(c_re[d][None], c_im[d][None], pwr[d][sel][:, :, None, :], pwi[d][sel][:, :, None, :])
        parts += [r_, -i_]
    wo = jnp.stack(parts)
    wo = wo.reshape(4, tc, nk, lg, h, p).transpose(2, 0, 5, 1, 3, 4).reshape(nk, 4, 1, p, tc * lg * h)
    out_group = jnp.asarray(((np.arange(tc * lg * h) // h) % lg)[None, :] == np.arange(lg)[:, None])
    wout = jnp.where(out_group[None, None, :, None, :], wo.astype(BF16), 0)
    wout = wout.reshape(nk, 4 * lg * p, tc * lg * h)

    a = jnp.stack([pwr[0, tc], pwi[0, tc], pwr[1, tc], pwi[1, tc]])
    a = a.reshape(4, nk, lg * p).transpose(1, 0, 2).reshape(nk, 1, 4 * lg * p)
    return tz, win, wout, a


def _merge_body(x_ref, ys_ref, of_ref, ob_ref, gate_ref, ya_ref, mg_ref, wm_ref, bm_ref, wglu_ref,
                gg_ref, wbs_ref, wbg_ref, wba_ref, wo_ref, o_ref):
    x = x_ref[...]
    d = x.shape[1]
    h = _rms(x, mg_ref[...]).astype(BF16)

    y = jax.nn.gelu(ys_ref[...])
    y_s5 = y * jax.nn.sigmoid(_dot(y.astype(BF16), wglu_ref[...]))

    o = of_ref[...] + ob_ref[...]
    heads = []
    for hh in range(GLA_HEADS):
        oh = o[:, hh * GLA_HEAD_DIM:(hh + 1) * GLA_HEAD_DIM]
        heads.append(oh * lax.rsqrt(jnp.mean(oh * oh, axis=-1, keepdims=True) + NORM_EPS))
    gate = gate_ref[...]
    y_gla = jnp.concatenate(heads, axis=-1) * gg_ref[...] * (gate * jax.nn.sigmoid(gate))

    merged = None
    for i, (val, w_ref) in enumerate(((y_s5.astype(BF16), wbs_ref), (y_gla.astype(BF16), wbg_ref),
                                      (ya_ref[...], wba_ref))):
        g = jax.nn.sigmoid(_dot(h, wm_ref[:, i * d:(i + 1) * d]) + bm_ref[:, i * d:(i + 1) * d])
        term = g * _dot(val, w_ref[...])
        merged = term if merged is None else merged + term
    o_ref[...] = x + _dot(merged.astype(BF16), wo_ref[...])


def _merge(x2d, ys5, o_f, o_b, proj2d, y_attn, mix_gain, w_merge, b_merge, w_glu, gla_gain,
           wb_s5, wb_gla, wb_attn, w_out, layer, *, tm=512):
    t, d = x2d.shape
    per_layer = lambda a: _layer(a.shape[1:], layer)
    row = lambda w: pl.BlockSpec((tm, w), lambda i: (i, 0))
    return pl.pallas_call(
        _merge_body,
        grid=(t // tm,),
        in_specs=[
            row(d), row(S5_WIDTH), row(GLA_WIDTH), row(GLA_WIDTH),
            pl.BlockSpec((tm, GLA_WIDTH), lambda i: (i, COL_GG // GLA_WIDTH)),
            row(ATTN_WIDTH),
            per_layer(mix_gain), per_layer(w_merge), per_layer(b_merge), per_layer(w_glu), per_layer(gla_gain),
            per_layer(wb_s5), per_layer(wb_gla), per_layer(wb_attn), per_layer(w_out),
        ],
        out_specs=row(d),
        out_shape=jax.ShapeDtypeStruct((t, d), F32),
        compiler_params=_cparams(("parallel",), 48),
        name="merge",
    )(x2d, ys5, o_f, o_b, proj2d, y_attn, mix_gain, w_merge, b_merge, w_glu, gla_gain,
      wb_s5, wb_gla, wb_attn, w_out)


def _reorder_w_in(w_in):
    z0 = COL_AQ
    z1 = z0 + 2 * GLA_LOWRANK
    pad = jnp.zeros(w_in.shape[:-1] + (PROJ_WIDTH - w_in.shape[-1],), w_in.dtype)
    return jnp.concatenate([w_in[..., :z0], w_in[..., z1:], w_in[..., z0:z1], pad], axis=-1)


def kernel(x, ffn1_norm, ffn1_w_gate, ffn1_w_up, ffn1_w_down, mix_norm, w_in, s5_lambda_re, s5_lambda_im, s5_log_dt, s5_b_re, s5_b_im, s5_c_re, s5_c_im, s5_d, s5_w_glu, gla_w_alpha, gla_b_alpha, gla_norm, attn_q_norm, attn_k_norm, w_branch_s5, w_branch_gla, w_branch_attn, w_merge_gate, b_merge_gate, w_out, ffn2_norm, ffn2_w_gate, ffn2_w_up, ffn2_w_down, final_norm):
    bsz, seq_len, d_model = x.shape
    depth = w_in.shape[0]
    tabs = _rope_tables(seq_len)
    bf = lambda w: w.astype(BF16)
    vec = lambda g: g.reshape(depth, 1, -1)
    ffn1 = (vec(ffn1_norm), bf(ffn1_w_gate), bf(ffn1_w_up), bf(ffn1_w_down))
    ffn2 = (vec(ffn2_norm), bf(ffn2_w_gate), bf(ffn2_w_up), bf(ffn2_w_down))
    mix_gain = vec(mix_norm)
    w_proj = bf(_reorder_w_in(w_in))
    s5_mats = jax.vmap(_s5_matrices)(s5_lambda_re, s5_lambda_im, s5_log_dt, s5_b_re, s5_b_im, s5_c_re, s5_c_im)
    s5_d_tiled = jnp.tile(s5_d.reshape(depth, S5_WIDTH // LANE, 1, LANE), (1, 1, 1, S5_CHUNK))
    gla_gates = _gla_gate_params(gla_w_alpha, gla_b_alpha)
    attn_gq = vec(jnp.tile(attn_q_norm, (1, ATTN_Q_HEADS)))
    attn_gk = vec(jnp.tile(attn_k_norm, (1, ATTN_KV_HEADS)))
    merge_params = (mix_gain, bf(w_merge_gate), vec(b_merge_gate), bf(s5_w_glu), vec(jnp.tile(gla_norm, (1, GLA_HEADS))),
                    bf(w_branch_s5), bf(w_branch_gla), bf(w_branch_attn), bf(w_out))

    flat = lambda a: a.reshape(bsz * seq_len, a.shape[-1])
    x2d = flat(x)
    for i in range(depth):
        x2d = _ffn(x2d, *ffn1, i)

        proj2d = _inproj(x2d, mix_gain, w_proj, i)
        proj = proj2d.reshape(bsz, seq_len, PROJ_WIDTH)
        ys5 = _s5(proj, s5_mats, s5_d_tiled, i)
        o_f, o_b = _gla(proj, gla_gates, i)
        y_attn = _attention(*_attn_prep(proj, tabs, attn_gq, attn_gk, i))
        x2d = _merge(x2d, flat(ys5), flat(o_f), flat(o_b), proj2d, flat(y_attn), *merge_params, i)

        x2d = _ffn(x2d, *ffn2, i, final_norm if i == depth - 1 else None)
    return x2d.reshape(bsz, seq_len, d_model)
```

```python
import functools
import math

import jax
import jax.numpy as jnp
import numpy as np
from jax import lax
from jax.experimental import pallas as pl
from jax.experimental.pallas import tpu as pltpu

F32 = jnp.float32
BF16 = jnp.bfloat16

NORM_EPS = 1e-6
S5_GROUPS = 32
S5_GROUP_CH = 16
S5_STATE = 64
S5_WIDTH = S5_GROUPS * S5_GROUP_CH
S5_CHUNK = 8
S5_LANE_GROUPS = 8
GLA_HEADS = 4
GLA_HEAD_DIM = 128
GLA_WIDTH = GLA_HEADS * GLA_HEAD_DIM
GLA_LOWRANK = 16
GLA_TAU = 16.0
GLA_CHUNK = 64
ATTN_Q_HEADS = 8
ATTN_KV_HEADS = 2
ATTN_HEAD_DIM = 64
ATTN_GROUP = ATTN_Q_HEADS // ATTN_KV_HEADS
ATTN_WIDTH = ATTN_Q_HEADS * ATTN_HEAD_DIM
ATTN_KV_WIDTH = ATTN_KV_HEADS * ATTN_HEAD_DIM
GRID_W = 64
ROPE_BASE = 10000.0
ROPE_PAIR = ATTN_HEAD_DIM // 4

LANE = 128
V7X_VMEM_BYTES = 64 * 1024 * 1024

COL_S5 = 0
COL_GQ = 512
COL_GK = 1024
COL_GV = 1536
COL_GG = 2048
COL_AQ = 2560
COL_AKV = 3072
COL_Z = 3328
PROJ_WIDTH = 3456


def _cparams(semantics, vmem_mib):
    return pltpu.CompilerParams(
        dimension_semantics=semantics,
        vmem_limit_bytes=min(vmem_mib * 1024 * 1024, V7X_VMEM_BYTES - 4 * 1024 * 1024),
    )


def _resident(shape):
    nd = len(shape)
    return pl.BlockSpec(shape, lambda *_: (0,) * nd, pipeline_mode=pl.Buffered(1))


def _layer(shape, layer):
    nd = len(shape)
    return pl.BlockSpec((None,) + tuple(shape), lambda *_: (layer,) + (0,) * nd, pipeline_mode=pl.Buffered(1))


def _rms(x, gain):
    ms = jnp.mean(x * x, axis=-1, keepdims=True)
    return x * lax.rsqrt(ms + NORM_EPS) * gain


def _dot(a, b):
    return jnp.dot(a, b, preferred_element_type=F32)


def _dot_nt(a, b):
    return lax.dot_general(a, b, (((1,), (1,)), ((), ())), preferred_element_type=F32)


def _dot_tn(a, b):
    return lax.dot_general(a, b, (((0,), (0,)), ((), ())), preferred_element_type=F32)


def _ffn_body(x_ref, gain_ref, wg_ref, wu_ref, wd_ref, *rest, chunk, final):
    if final:
        fg_ref, o_ref, a_ref = rest
    else:
        o_ref, a_ref = rest
    x = x_ref[...]
    h = _rms(x, gain_ref[...]).astype(BF16)
    d_ff = wg_ref.shape[1]
    for c0 in range(0, d_ff, chunk):
        g = _dot(h, wg_ref[:, c0:c0 + chunk])
        u = _dot(h, wu_ref[:, c0:c0 + chunk])
        a_ref[:, c0:c0 + chunk] = (g * jax.nn.sigmoid(g) * u).astype(BF16)
    out = x + 0.5 * _dot(a_ref[...], wd_ref[...])
    if final:
        out = _rms(out, fg_ref[...])
    o_ref[...] = out


def _ffn(x2d, gain, wg, wu, wd, layer, final_gain=None, *, tm=512, chunk=256):
    t, d = x2d.shape
    f = wg.shape[2]
    final = final_gain is not None
    row = pl.BlockSpec((tm, d), lambda i: (i, 0))
    in_specs = [row, _layer((1, d), layer), _layer((d, f), layer), _layer((d, f), layer), _layer((f, d), layer)]
    args = [x2d, gain, wg, wu, wd]
    if final:
        in_specs.append(_resident((1, d)))
        args.append(final_gain.reshape(1, d))
    return pl.pallas_call(
        functools.partial(_ffn_body, chunk=chunk, final=final),
        grid=(t // tm,),
        in_specs=in_specs,
        out_specs=row,
        out_shape=jax.ShapeDtypeStruct((t, d), F32),
        scratch_shapes=[pltpu.VMEM((tm, f), BF16)],
        compiler_params=_cparams(("parallel",), 48),
        name="ffn",
    )(*args)


def _inproj_body(x_ref, gain_ref, w_ref, o_ref, *, chunk):
    h = _rms(x_ref[...], gain_ref[...]).astype(BF16)
    n = w_ref.shape[1]
    for c0 in range(0, n, chunk):
        c1 = min(c0 + chunk, n)
        o_ref[:, c0:c1] = _dot(h, w_ref[:, c0:c1])


def _inproj(x2d, gain, w, layer, *, tm=512, chunk=512):
    t, d = x2d.shape
    n = w.shape[2]
    return pl.pallas_call(
        functools.partial(_inproj_body, chunk=chunk),
        grid=(t // tm,),
        in_specs=[pl.BlockSpec((tm, d), lambda i: (i, 0)), _layer((1, d), layer), _layer((d, n), layer)],
        out_specs=pl.BlockSpec((tm, n), lambda i: (i, 0)),
        out_shape=jax.ShapeDtypeStruct((t, n), F32),
        compiler_params=_cparams(("parallel",), 40),
        name="inproj",
    )(x2d, gain, w)


def _head_rms(x, ones_ref, gain):
    x2 = x * x
    hi = x2.astype(BF16)
    lo = (x2 - hi.astype(F32)).astype(BF16)
    ss = _dot(hi, ones_ref[...]) + _dot(lo, ones_ref[...])
    return x * lax.rsqrt(ss * (1.0 / ATTN_HEAD_DIM) + NORM_EPS) * gain


def _rope(x, cos, sin_signed):
    n = x.shape[-1]
    lane = lax.broadcasted_iota(jnp.int32, x.shape, 1)
    first = (lane & ROPE_PAIR) == 0
    partner = jnp.where(first, pltpu.roll(x, n - ROPE_PAIR, 1), pltpu.roll(x, ROPE_PAIR, 1))
    return x * cos + partner * sin_signed


def _attn_prep_body(q_ref, kv_ref, cq_ref, sq_ref, ck_ref, sk_ref, gq_ref, gk_ref, oq_ref, ok_ref,
                    qt_ref, k_ref, vt_ref):
    q = _rope(_head_rms(q_ref[0], oq_ref, gq_ref[...]), cq_ref[...], sq_ref[...])
    qt_ref[0] = q.T.astype(BF16)
    kv = kv_ref[0]
    k = _rope(_head_rms(kv[:, :ATTN_KV_WIDTH], ok_ref, gk_ref[...]), ck_ref[...], sk_ref[...])
    for g in range(ATTN_KV_HEADS):
        k_ref[0, g] = k[:, g * ATTN_HEAD_DIM:(g + 1) * ATTN_HEAD_DIM].astype(BF16)
    vt_ref[0] = kv[:, ATTN_KV_WIDTH:].T.astype(BF16)


def _attn_prep(proj, tabs, gq, gk, layer, *, tm=512):
    b, l, _ = proj.shape
    cq, sq, ck, sk = tabs
    ones_q = jnp.asarray(np.kron(np.eye(ATTN_Q_HEADS), np.ones((ATTN_HEAD_DIM, ATTN_HEAD_DIM))), BF16)
    ones_k = jnp.asarray(np.kron(np.eye(ATTN_KV_HEADS), np.ones((ATTN_HEAD_DIM, ATTN_HEAD_DIM))), BF16)
    kvw = 2 * ATTN_KV_WIDTH
    tab_q = pl.BlockSpec((tm, ATTN_WIDTH), lambda bi, i: (i, 0))
    tab_k = pl.BlockSpec((tm, ATTN_KV_WIDTH), lambda bi, i: (i, 0))
    return pl.pallas_call(
        _attn_prep_body,
        grid=(b, l // tm),
        in_specs=[
            pl.BlockSpec((1, tm, ATTN_WIDTH), lambda bi, i: (bi, i, COL_AQ // ATTN_WIDTH)),
            pl.BlockSpec((1, tm, kvw), lambda bi, i: (bi, i, COL_AKV // kvw)),
            tab_q, tab_q, tab_k, tab_k,
            _layer((1, ATTN_WIDTH), layer), _layer((1, ATTN_KV_WIDTH), layer),
            _resident((ATTN_WIDTH, ATTN_WIDTH)), _resident((ATTN_KV_WIDTH, ATTN_KV_WIDTH)),
        ],
        out_specs=[
            pl.BlockSpec((1, ATTN_WIDTH, tm), lambda bi, i: (bi, 0, i)),
            pl.BlockSpec((1, ATTN_KV_HEADS, tm, ATTN_HEAD_DIM), lambda bi, i: (bi, 0, i, 0)),
            pl.BlockSpec((1, ATTN_KV_WIDTH, tm), lambda bi, i: (bi, 0, i)),
        ],
        out_shape=[
            jax.ShapeDtypeStruct((b, ATTN_WIDTH, l), BF16),
            jax.ShapeDtypeStruct((b, ATTN_KV_HEADS, l, ATTN_HEAD_DIM), BF16),
            jax.ShapeDtypeStruct((b, ATTN_KV_WIDTH, l), BF16),
        ],
        compiler_params=_cparams(("parallel", "parallel"), 32),
        name="attn_prep",
    )(proj, proj, cq, sq, ck, sk, gq, gk, ones_q, ones_k)


def _rope_tables(l):
    t = jnp.arange(l, dtype=jnp.int32)
    rows = (t // GRID_W).astype(F32)
    cols = (t % GRID_W).astype(F32)
    half = ATTN_HEAD_DIM // 2
    inv_freq = ROPE_BASE ** (-jnp.arange(half // 2, dtype=F32) * 2.0 / half)
    d = np.arange(ATTN_HEAD_DIM)
    use_rows = jnp.asarray((d // half) == 0)
    freq = inv_freq[jnp.asarray(d % (half // 2))]
    second = jnp.asarray((d % half) >= half // 2)
    ang = jnp.where(use_rows[None, :], rows[:, None], cols[:, None]) * freq[None, :]
    cos = jnp.cos(ang)
    sin = jnp.sin(ang)
    sin = jnp.where(second[None, :], sin, -sin)
    scale = ATTN_HEAD_DIM ** -0.5 * math.log2(math.e)
    return (jnp.tile(cos, (1, ATTN_Q_HEADS)) * scale, jnp.tile(sin, (1, ATTN_Q_HEADS)) * scale,
            jnp.tile(cos, (1, ATTN_KV_HEADS)), jnp.tile(sin, (1, ATTN_KV_HEADS)))


def _attn_body(qt_ref, k_ref, vt_ref, o_ref, *, kb):
    n_kb = k_ref.shape[2] // kb
    items = [(h, j) for h in range(ATTN_GROUP) for j in range(n_kb)]

    def scores(h, j):
        qt = qt_ref[0, h * ATTN_HEAD_DIM:(h + 1) * ATTN_HEAD_DIM, :]
        return _dot(k_ref[0, 0, j * kb:(j + 1) * kb, :], qt)

    outs = []
    st = scores(*items[0])
    run_max = run_sum = acc = None
    for idx, (h, j) in enumerate(items):
        st_next = scores(*items[idx + 1]) if idx + 1 < len(items) else None
        blk_max = jnp.max(st, axis=0, keepdims=True)
        new_max = blk_max if j == 0 else jnp.maximum(run_max, blk_max)
        p = jnp.exp2(st - new_max)
        blk_sum = jnp.sum(p, axis=0, keepdims=True)
        blk_acc = _dot(vt_ref[0, :, j * kb:(j + 1) * kb], p.astype(BF16))
        if j == 0:
            run_sum, acc = blk_sum, blk_acc
        else:
            alpha = jnp.exp2(run_max - new_max)
            run_sum = alpha * run_sum + blk_sum
            acc = alpha * acc + blk_acc
        run_max = new_max
        if j == n_kb - 1:
            outs.append((acc / run_sum).T)
        st = st_next
    o_ref[0] = jnp.concatenate(outs, axis=-1).astype(o_ref.dtype)


def _attention(qt, k, vt, *, tq=1024, kb=512):
    b, _, l = qt.shape
    tq, kb = min(tq, l), min(kb, l)
    assert l % tq == 0 and l % kb == 0
    gw = ATTN_GROUP * ATTN_HEAD_DIM
    return pl.pallas_call(
        functools.partial(_attn_body, kb=kb),
        grid=(b, ATTN_KV_HEADS, l // tq),
        in_specs=[
            pl.BlockSpec((1, gw, tq), lambda bi, g, i: (bi, g, i)),
            pl.BlockSpec((1, 1, l, ATTN_HEAD_DIM), lambda bi, g, i: (bi, g, 0, 0)),
            pl.BlockSpec((1, ATTN_HEAD_DIM, l), lambda bi, g, i: (bi, g, 0)),
        ],
        out_specs=pl.BlockSpec((1, tq, gw), lambda bi, g, i: (bi, i, g)),
        out_shape=jax.ShapeDtypeStruct((b, l, ATTN_WIDTH), BF16),
        compiler_params=_cparams(("parallel", "parallel", "parallel"), 48),
        name="attention",
    )(qt, k, vt)


def _gla_body(qf_ref, kf_ref, vf_ref, zf_ref, qb_ref, kb_ref, vb_ref, zb_ref,
              waf_ref, wab_ref, baf_ref, bab_ref, cumf_ref, cumb_ref,
              of_ref, ob_ref, st_ref):
    @pl.when(pl.program_id(1) == 0)
    def _():
        st_ref[...] = jnp.zeros_like(st_ref)

    tb = qf_ref.shape[1]
    n_chunks = tb // GLA_CHUNK
    both = (0, 1)
    heads = [slice(h * GLA_HEAD_DIM, (h + 1) * GLA_HEAD_DIM) for h in range(GLA_HEADS)]
    chunks = [slice(ci * GLA_CHUNK, (ci + 1) * GLA_CHUNK) for ci in range(n_chunks)]
    q_refs, k_refs, v_refs, z_refs = (qf_ref, qb_ref), (kf_ref, kb_ref), (vf_ref, vb_ref), (zf_ref, zb_ref)
    wa_refs, ba_refs, cum_refs, o_refs = (waf_ref, wab_ref), (baf_ref, bab_ref), (cumf_ref, cumb_ref), (of_ref, ob_ref)

    logits = [_dot(z_refs[d][0].astype(BF16), wa_refs[d][...]) + ba_refs[d][...] for d in both]
    log_a = [(jnp.minimum(x, 0.0) - jnp.log(1.0 + jnp.exp(-jnp.abs(x)))) * (1.0 / GLA_TAU) for x in logits]
    hi = [x.astype(BF16) for x in log_a]
    lo = [(x - h.astype(F32)).astype(BF16) for x, h in zip(log_a, hi)]
    bcum = [_dot(cum_refs[d][...], hi[d]) + _dot(cum_refs[d][...], lo[d]) for d in both]
    q_dec, k_dec, k_end, decay, vb, mask = [], [], [], [], [], []
    r = lax.broadcasted_iota(jnp.int32, (tb, tb), 0)
    c = lax.broadcasted_iota(jnp.int32, (tb, tb), 1)
    same = (r // GLA_CHUNK) == (c // GLA_CHUNK)
    for d in both:
        edge = 0 if d else GLA_CHUNK - 1
        btot = jnp.concatenate(
            [jnp.broadcast_to(bcum[d][rows.start + edge:rows.start + edge + 1], (GLA_CHUNK, bcum[d].shape[1]))
             for rows in chunks], axis=0)
        k = k_refs[d][0]
        q_dec.append(((q_refs[d][0] * GLA_HEAD_DIM ** -0.5) * jnp.exp(bcum[d])).astype(BF16))
        k_dec.append((k * jnp.exp(-bcum[d])).astype(BF16))
        k_end.append((k * jnp.exp(btot - bcum[d])).astype(BF16))
        decay.append(jnp.exp(btot))
        vb.append(v_refs[d][0].astype(BF16))
        mask.append(same & ((c >= r) if d else (c <= r)))

    scores = [[jnp.where(mask[d], _dot_nt(q_dec[d][:, hs], k_dec[d][:, hs]), 0.0).astype(BF16) for hs in heads]
              for d in both]
    o_intra = [[_dot(scores[d][h], vb[d][:, hs]) for h, hs in enumerate(heads)] for d in both]
    kv = [[[_dot_tn(vb[d][rows, hs], k_end[d][rows, hs]) for rows in chunks] for hs in heads] for d in both]

    enter = [[[None] * n_chunks for _ in heads] for _ in both]
    for d in both:
        order = range(n_chunks - 1, -1, -1) if d else range(n_chunks)
        for h, hs in enumerate(heads):
            st = st_ref[d, h]
            for ci in order:
                enter[d][h][ci] = st.astype(BF16)
                st = decay[d][chunks[ci].start:chunks[ci].start + 1, hs] * st + kv[d][h][ci]
            st_ref[d, h] = st

    o_inter = [[[_dot_nt(q_dec[d][rows, hs], enter[d][h][ci]) for ci, rows in enumerate(chunks)]
                for h, hs in enumerate(heads)] for d in both]
    for d in both:
        o_refs[d][0] = jnp.concatenate(
            [o_intra[d][h] + jnp.concatenate(o_inter[d][h], axis=0) for h in range(GLA_HEADS)], axis=-1)


def _gla_gate_params(w_alpha, b_alpha):
    depth = w_alpha.shape[0]
    r = GLA_LOWRANK
    zeros = jnp.zeros((depth, LANE, GLA_WIDTH), F32)
    waf = zeros.at[:, :r].set(w_alpha[:, 0]).astype(BF16)
    wab = zeros.at[:, r:2 * r].set(w_alpha[:, 1]).astype(BF16)
    return waf, wab, b_alpha[:, 0].reshape(depth, 1, GLA_WIDTH), b_alpha[:, 1].reshape(depth, 1, GLA_WIDTH)


def _gla(proj, gate_params, layer, *, tb=256):
    b, l, _ = proj.shape
    nblk = l // tb
    idx = np.arange(tb)
    same = (idx[:, None] // GLA_CHUNK) == (idx[None, :] // GLA_CHUNK)
    cum_f = jnp.asarray(same & (idx[None, :] <= idx[:, None]), BF16)
    cum_b = jnp.asarray(same & (idx[None, :] >= idx[:, None]), BF16)

    def col(base, width, rev):
        if rev:
            return pl.BlockSpec((1, tb, width), lambda bi, i: (bi, nblk - 1 - i, base // width))
        return pl.BlockSpec((1, tb, width), lambda bi, i: (bi, i, base // width))

    w = GLA_WIDTH
    in_specs = [col(COL_GQ, w, False), col(COL_GK, w, False), col(COL_GV, w, False), col(COL_Z, LANE, False),
                col(COL_GQ, w, True), col(COL_GK, w, True), col(COL_GV, w, True), col(COL_Z, LANE, True),
                _layer((LANE, w), layer), _layer((LANE, w), layer), _layer((1, w), layer), _layer((1, w), layer),
                _resident((tb, tb)), _resident((tb, tb))]
    return pl.pallas_call(
        _gla_body,
        grid=(b, nblk),
        in_specs=in_specs,
        out_specs=[col(0, w, False), col(0, w, True)],
        out_shape=[jax.ShapeDtypeStruct((b, l, GLA_WIDTH), F32)] * 2,
        scratch_shapes=[pltpu.VMEM((2, GLA_HEADS, GLA_HEAD_DIM, GLA_HEAD_DIM), F32)],
        compiler_params=_cparams(("parallel", "arbitrary"), 32),
        name="gla",
    )(*([proj] * 8), *gate_params, cum_f, cum_b)


def _s5_expand(comb_ref, winc_ref, woc_ref, tz_scr, win_scr, wout_scr):
    tc, lg, h, p = S5_CHUNK, S5_LANE_GROUPS, S5_GROUP_CH, S5_STATE
    nst = lg * p
    row_g = lax.broadcasted_iota(jnp.int32, (LANE, 1), 0) // h
    same = row_g == lax.broadcasted_iota(jnp.int32, (1, LANE), 1) // h
    blocks = [jnp.where(same, jnp.concatenate([comb_ref[lag]] * lg, axis=0), 0.0).astype(BF16)
              for lag in range(2 * tc - 1)]
    for tp in range(tc):
        for t in range(tc):
            tz_scr[tp * LANE:(tp + 1) * LANE, t * LANE:(t + 1) * LANE] = blocks[t - tp + tc - 1]
    same = row_g == (lax.broadcasted_iota(jnp.int32, (1, 4 * nst), 1) % nst) // p
    for tp in range(tc):
        slab = winc_ref[tp * h:(tp + 1) * h, :]
        win_scr[tp * LANE:(tp + 1) * LANE, :] = jnp.where(
            same, jnp.concatenate([slab] * lg, axis=0), 0.0).astype(BF16)
    lane_g = (lax.broadcasted_iota(jnp.int32, (1, tc * LANE), 1) // h) % lg
    for part in range(4):
        tab = woc_ref[part * p:(part + 1) * p, :]
        for g in range(lg):
            r0 = (part * lg + g) * p
            wout_scr[r0:r0 + p, :] = jnp.where(lane_g == g, tab, 0.0).astype(BF16)


def _s5_body(u_ref, comb_ref, winc_ref, woc_ref, a_ref, d_ref, o_ref, tz_scr, win_scr, wout_scr, z_scr, x_scr):
    @pl.when(pl.program_id(1) == 0)
    def _():
        _s5_expand(comb_ref, winc_ref, woc_ref, tz_scr, win_scr, wout_scr)

    l = u_ref.shape[1]
    nc = l // S5_CHUNK
    nst = S5_LANE_GROUPS * S5_STATE
    uk = jnp.concatenate([u_ref[0, pl.ds(t, nc, stride=S5_CHUNK), :] for t in range(S5_CHUNK)], axis=-1)
    ukb = uk.astype(BF16)
    z_scr[...] = _dot(ukb, win_scr[...])
    y_local = _dot(ukb, tz_scr[...]) + uk * d_ref[...]

    a = a_ref[...]
    afr, afi, abr, abi = (a[:, i * nst:(i + 1) * nst] for i in range(4))
    ntile = nc // 8

    def tile_step(i, carry):
        sfr, sfi, sbr, sbi = carry
        rf = pl.multiple_of(i * 8, 8)
        rb = pl.multiple_of((ntile - 1 - i) * 8, 8)
        zf = z_scr[pl.ds(rf, 8), 0:2 * nst]
        zb = z_scr[pl.ds(rb, 8), 2 * nst:4 * nst]
        xfr, xfi, xbr, xbi = [], [], [None] * 8, [None] * 8
        for r in range(8):
            xfr.append(sfr)
            xfi.append(sfi)
            sfr, sfi = (afr * sfr - afi * sfi + zf[r:r + 1, :nst],
                        afr * sfi + afi * sfr + zf[r:r + 1, nst:])
            q = 7 - r
            xbr[q] = sbr
            xbi[q] = sbi
            sbr, sbi = (abr * sbr - abi * sbi + zb[q:q + 1, :nst],
                        abr * sbi + abi * sbr + zb[q:q + 1, nst:])
        x_scr[pl.ds(rf, 8), 0:nst] = jnp.concatenate(xfr, axis=0)
        x_scr[pl.ds(rf, 8), nst:2 * nst] = jnp.concatenate(xfi, axis=0)
        x_scr[pl.ds(rb, 8), 2 * nst:3 * nst] = jnp.concatenate(xbr, axis=0)
        x_scr[pl.ds(rb, 8), 3 * nst:4 * nst] = jnp.concatenate(xbi, axis=0)
        return sfr, sfi, sbr, sbi

    zero = jnp.zeros((1, nst), F32)
    lax.fori_loop(0, ntile, tile_step, (zero, zero, zero, zero), unroll=True)

    y = y_local + _dot(x_scr[...].astype(BF16), wout_scr[...])
    for t in range(S5_CHUNK):
        o_ref[0, pl.ds(t, nc, stride=S5_CHUNK), :] = y[:, t * LANE:(t + 1) * LANE]


def _s5(proj, tables, d_tiled, layer):
    b, l, _ = proj.shape
    comb, winc, woc, a = tables
    nk = S5_WIDTH // LANE
    nc = l // S5_CHUNK
    feat = S5_CHUNK * LANE
    nst4 = 4 * S5_LANE_GROUPS * S5_STATE
    per_block = lambda arr: pl.BlockSpec((None, None) + arr.shape[2:], lambda k, bi: (layer, k, 0, 0))
    return pl.pallas_call(
        _s5_body,
        grid=(nk, b),
        in_specs=[
            pl.BlockSpec((1, l, LANE), lambda k, bi: (bi, 0, COL_S5 // LANE + k)),
            pl.BlockSpec((None,) + comb.shape[1:3] + (LANE,), lambda k, bi: (layer, 0, 0, k)),
            per_block(winc), per_block(woc), per_block(a), per_block(d_tiled),
        ],
        out_specs=pl.BlockSpec((1, l, LANE), lambda k, bi: (bi, 0, k)),
        out_shape=jax.ShapeDtypeStruct((b, l, S5_WIDTH), F32),
        scratch_shapes=[pltpu.VMEM((feat, feat), BF16), pltpu.VMEM((feat, nst4), BF16), pltpu.VMEM((nst4, feat), BF16),
                        pltpu.VMEM((nc, nst4), F32), pltpu.VMEM((nc, nst4), F32)],
        compiler_params=_cparams(("arbitrary", "arbitrary"), 48),
        name="s5",
    )(proj, comb, winc, woc, a, d_tiled)


def _cmul(ar, ai, br, bi):
    return ar * br - ai * bi, ar * bi + ai * br


def _s5_tables(lam_re, lam_im, log_dt, b_re, b_im, c_re, c_im):
    g, p, h, tc, lg = S5_GROUPS, S5_STATE, S5_GROUP_CH, S5_CHUNK, S5_LANE_GROUPS
    nk = g // lg
    dt = jnp.exp(log_dt)[..., None]
    mag = jnp.exp(lam_re * dt)
    lbr, lbi = mag * jnp.cos(lam_im * dt), mag * jnp.sin(lam_im * dt)
    den = lam_re * lam_re + lam_im * lam_im
    nr, ni = lbr - 1.0, lbi
    fr, fi = (nr * lam_re + ni * lam_im) / den, (ni * lam_re - nr * lam_im) / den
    bbr, bbi = _cmul(fr[..., None], fi[..., None], b_re, b_im)
    pwr, pwi = [jnp.ones_like(lbr)], [jnp.zeros_like(lbi)]
    for _ in range(tc):
        nr_, ni_ = _cmul(pwr[-1], pwi[-1], lbr, lbi)
        pwr.append(nr_)
        pwi.append(ni_)
    pwr, pwi = jnp.stack(pwr, 1), jnp.stack(pwi, 1)

    kern = []
    for d in range(2):
        lanes = lambda x: jnp.repeat(x, h, axis=-1)
        cr = c_re[d].transpose(2, 0, 1).reshape(p, 1, 1, g * h)
        ci = c_im[d].transpose(2, 0, 1).reshape(p, 1, 1, g * h)
        pr = lanes(pwr[d, :tc].transpose(2, 0, 1)).reshape(p, tc, 1, g * h)
        pi = lanes(pwi[d, :tc].transpose(2, 0, 1)).reshape(p, tc, 1, g * h)
        br = lanes(bbr[d].transpose(1, 2, 0)).reshape(p, 1, h, g * h)
        bi = lanes(bbi[d].transpose(1, 2, 0)).reshape(p, 1, h, g * h)
        mr, mi = _cmul(cr, ci, pr, pi)
        kern.append(jnp.sum(mr * br - mi * bi, axis=0))
    comb = jnp.concatenate([kern[1][tc - 1:0:-1], (kern[0][0] + kern[1][0])[None], kern[0][1:tc]])

    parts = []
    for d, sel in ((0, np.arange(tc - 1, -1, -1)), (1, np.arange(tc))):
        r_, i_ = _cmul(pwr[d][sel][..., None], pwi[d][sel][..., None], bbr[d][None], bbi[d][None])
        parts += [r_, i_]
    w = jnp.stack([x.transpose(0, 3, 1, 2).reshape(tc, h, nk, lg * p) for x in parts])
    winc = w.transpose(3, 1, 2, 0, 4).reshape(nk, tc * h, 4 * lg * p)

    parts = []
    for d, sel in ((0, np.arange(1, tc + 1)), (1, np.arange(tc, 0, -1))):
        r_, i_ = _cmul(c_re[d][None], c_im[d][None], pwr[d][sel][:, :, None, :], pwi[d][sel][:, :, None, :])
        parts += [r_, -i_]
    wo = jnp.stack([x.reshape(tc, nk, lg, h, p).transpose(1, 4, 0, 2, 3).reshape(nk, p, tc * lg * h) for x in parts], 1)
    woc = wo.reshape(nk, 4 * p, tc * lg * h)

    a = jnp.stack([pwr[0, tc], pwi[0, tc], pwr[1, tc], pwi[1, tc]])
    a = a.reshape(4, nk, lg * p).transpose(1, 0, 2).reshape(nk, 1, 4 * lg * p)
    return comb, winc, woc, a


def _merge_body(x_ref, ys_ref, of_ref, ob_ref, gate_ref, ya_ref, mg_ref, wm_ref, bm_ref, wglu_ref,
                gg_ref, wbs_ref, wbg_ref, wba_ref, wo_ref, o_ref):
    x = x_ref[...]
    d = x.shape[1]
    h = _rms(x, mg_ref[...]).astype(BF16)

    y = jax.nn.gelu(ys_ref[...])
    y_s5 = y * jax.nn.sigmoid(_dot(y.astype(BF16), wglu_ref[...]))

    o = of_ref[...] + ob_ref[...]
    heads = []
    for hh in range(GLA_HEADS):
        oh = o[:, hh * GLA_HEAD_DIM:(hh + 1) * GLA_HEAD_DIM]
        heads.append(oh * lax.rsqrt(jnp.mean(oh * oh, axis=-1, keepdims=True) + NORM_EPS))
    gate = gate_ref[...]
    y_gla = jnp.concatenate(heads, axis=-1) * gg_ref[...] * (gate * jax.nn.sigmoid(gate))

    merged = None
    for i, (val, w_ref) in enumerate(((y_s5.astype(BF16), wbs_ref), (y_gla.astype(BF16), wbg_ref),
                                      (ya_ref[...], wba_ref))):
        g = jax.nn.sigmoid(_dot(h, wm_ref[:, i * d:(i + 1) * d]) + bm_ref[:, i * d:(i + 1) * d])
        term = g * _dot(val, w_ref[...])
        merged = term if merged is None else merged + term
    o_ref[...] = x + _dot(merged.astype(BF16), wo_ref[...])


def _merge(x2d, ys5, o_f, o_b, proj2d, y_attn, mix_gain, w_merge, b_merge, w_glu, gla_gain,
           wb_s5, wb_gla, wb_attn, w_out, layer, *, tm=512):
    t, d = x2d.shape
    per_layer = lambda a: _layer(a.shape[1:], layer)
    row = lambda w: pl.BlockSpec((tm, w), lambda i: (i, 0))
    return pl.pallas_call(
        _merge_body,
        grid=(t // tm,),
        in_specs=[
            row(d), row(S5_WIDTH), row(GLA_WIDTH), row(GLA_WIDTH),
            pl.BlockSpec((tm, GLA_WIDTH), lambda i: (i, COL_GG // GLA_WIDTH)),
            row(ATTN_WIDTH),
            per_layer(mix_gain), per_layer(w_merge), per_layer(b_merge), per_layer(w_glu), per_layer(gla_gain),
            per_layer(wb_s5), per_layer(wb_gla), per_layer(wb_attn), per_layer(w_out),
        ],
        out_specs=row(d),
        out_shape=jax.ShapeDtypeStruct((t, d), F32),
        compiler_params=_cparams(("parallel",), 48),
        name="merge",
    )(x2d, ys5, o_f, o_b, proj2d, y_attn, mix_gain, w_merge, b_merge, w_glu, gla_gain,
      wb_s5, wb_gla, wb_attn, w_out)


def _reorder_w_in(w_in):
    z0 = COL_AQ
    z1 = z0 + 2 * GLA_LOWRANK
    pad = jnp.zeros(w_in.shape[:-1] + (PROJ_WIDTH - w_in.shape[-1],), w_in.dtype)
    return jnp.concatenate([w_in[..., :z0], w_in[..., z1:], w_in[..., z0:z1], pad], axis=-1)


def kernel(x, ffn1_norm, ffn1_w_gate, ffn1_w_up, ffn1_w_down, mix_norm, w_in, s5_lambda_re, s5_lambda_im, s5_log_dt, s5_b_re, s5_b_im, s5_c_re, s5_c_im, s5_d, s5_w_glu, gla_w_alpha, gla_b_alpha, gla_norm, attn_q_norm, attn_k_norm, w_branch_s5, w_branch_gla, w_branch_attn, w_merge_gate, b_merge_gate, w_out, ffn2_norm, ffn2_w_gate, ffn2_w_up, ffn2_w_down, final_norm):
    bsz, seq_len, d_model = x.shape
    depth = w_in.shape[0]
    tabs = _rope_tables(seq_len)
    bf = lambda w: w.astype(BF16)
    vec = lambda g: g.reshape(depth, 1, -1)
    ffn1 = (vec(ffn1_norm), bf(ffn1_w_gate), bf(ffn1_w_up), bf(ffn1_w_down))
    ffn2 = (vec(ffn2_norm), bf(ffn2_w_gate), bf(ffn2_w_up), bf(ffn2_w_down))
    mix_gain = vec(mix_norm)
    w_proj = bf(_reorder_w_in(w_in))
    s5_tabs = jax.vmap(_s5_tables)(s5_lambda_re, s5_lambda_im, s5_log_dt, s5_b_re, s5_b_im, s5_c_re, s5_c_im)
    s5_d_tiled = jnp.tile(s5_d.reshape(depth, S5_WIDTH // LANE, 1, LANE), (1, 1, 1, S5_CHUNK))
    gla_gates = _gla_gate_params(gla_w_alpha, gla_b_alpha)
    attn_gq = vec(jnp.tile(attn_q_norm, (1, ATTN_Q_HEADS)))
    attn_gk = vec(jnp.tile(attn_k_norm, (1, ATTN_KV_HEADS)))
    merge_params = (mix_gain, bf(w_merge_gate), vec(b_merge_gate), bf(s5_w_glu), vec(jnp.tile(gla_norm, (1, GLA_HEADS))),
                    bf(w_branch_s5), bf(w_branch_gla), bf(w_branch_attn), bf(w_out))

    flat = lambda a: a.reshape(bsz * seq_len, a.shape[-1])
    x2d = flat(x)
    for i in range(depth):
        x2d = _ffn(x2d, *ffn1, i)

        proj2d = _inproj(x2d, mix_gain, w_proj, i)
        proj = proj2d.reshape(bsz, seq_len, PROJ_WIDTH)
        ys5 = _s5(proj, s5_tabs, s5_d_tiled, i)
        o_f, o_b = _gla(proj, gla_gates, i)
        y_attn = _attention(*_attn_prep(proj, tabs, attn_gq, attn_gk, i))
        x2d = _merge(x2d, flat(ys5), flat(o_f), flat(o_b), proj2d, flat(y_attn), *merge_params, i)

        x2d = _ffn(x2d, *ffn2, i, final_norm if i == depth - 1 else None)
    return x2d.reshape(bsz, seq_len, d_model)
```

```python
import functools
import math

import jax
import jax.numpy as jnp
import numpy as np
from jax import lax
from jax.experimental import pallas as pl
from jax.experimental.pallas import tpu as pltpu

F32 = jnp.float32
BF16 = jnp.bfloat16

NORM_EPS = 1e-6
S5_GROUPS = 32
S5_GROUP_CH = 16
S5_STATE = 64
S5_WIDTH = S5_GROUPS * S5_GROUP_CH
S5_CHUNK = 8
S5_LANE_GROUPS = 8
GLA_HEADS = 4
GLA_HEAD_DIM = 128
GLA_WIDTH = GLA_HEADS * GLA_HEAD_DIM
GLA_LOWRANK = 16
GLA_TAU = 16.0
GLA_CHUNK = 64
ATTN_Q_HEADS = 8
ATTN_KV_HEADS = 2
ATTN_HEAD_DIM = 64
ATTN_GROUP = ATTN_Q_HEADS // ATTN_KV_HEADS
ATTN_WIDTH = ATTN_Q_HEADS * ATTN_HEAD_DIM
ATTN_KV_WIDTH = ATTN_KV_HEADS * ATTN_HEAD_DIM
GRID_W = 64
ROPE_BASE = 10000.0
ROPE_PAIR = ATTN_HEAD_DIM // 4
ATTN_VT_ROWS = ATTN_HEAD_DIM + 16

LANE = 128
V7X_VMEM_BYTES = 64 * 1024 * 1024

COL_S5 = 0
COL_GQ = 512
COL_GK = 1024
COL_GV = 1536
COL_GG = 2048
COL_Z = 2560
PROJ_WIDTH = 2688


def _cparams(semantics, vmem_mib):
    return pltpu.CompilerParams(
        dimension_semantics=semantics,
        vmem_limit_bytes=min(vmem_mib * 1024 * 1024, V7X_VMEM_BYTES - 4 * 1024 * 1024),
    )


def _resident(shape):
    nd = len(shape)
    return pl.BlockSpec(shape, lambda *_: (0,) * nd, pipeline_mode=pl.Buffered(1))


def _layer(shape, layer):
    nd = len(shape)
    return pl.BlockSpec((None,) + tuple(shape), lambda *_: (layer,) + (0,) * nd, pipeline_mode=pl.Buffered(1))


def _rms(x, gain):
    ms = jnp.mean(x * x, axis=-1, keepdims=True)
    return x * lax.rsqrt(ms + NORM_EPS) * gain


def _dot(a, b):
    return jnp.dot(a, b, preferred_element_type=F32)


def _dot_nt(a, b):
    return lax.dot_general(a, b, (((1,), (1,)), ((), ())), preferred_element_type=F32)


def _dot_tn(a, b):
    return lax.dot_general(a, b, (((0,), (0,)), ((), ())), preferred_element_type=F32)


def _ffn_body(x_ref, gain_ref, wg_ref, wu_ref, wd_ref, *rest, chunk, final):
    if final:
        fg_ref, o_ref, a_ref = rest
    else:
        o_ref, a_ref = rest
    x = x_ref[...]
    h = _rms(x, gain_ref[...]).astype(BF16)
    d_ff = wg_ref.shape[1]
    for c0 in range(0, d_ff, chunk):
        g = _dot(h, wg_ref[:, c0:c0 + chunk])
        u = _dot(h, wu_ref[:, c0:c0 + chunk])
        a_ref[:, c0:c0 + chunk] = (g * jax.nn.sigmoid(g) * u).astype(BF16)
    out = x + 0.5 * _dot(a_ref[...], wd_ref[...])
    if final:
        out = _rms(out, fg_ref[...])
    o_ref[...] = out


def _ffn(x2d, gain, wg, wu, wd, layer, final_gain=None, *, tm=512, chunk=256):
    t, d = x2d.shape
    f = wg.shape[2]
    final = final_gain is not None
    row = pl.BlockSpec((tm, d), lambda i: (i, 0))
    in_specs = [row, _layer((1, d), layer), _layer((d, f), layer), _layer((d, f), layer), _layer((f, d), layer)]
    args = [x2d, gain, wg, wu, wd]
    if final:
        in_specs.append(_resident((1, d)))
        args.append(final_gain.reshape(1, d))
    return pl.pallas_call(
        functools.partial(_ffn_body, chunk=chunk, final=final),
        grid=(t // tm,),
        in_specs=in_specs,
        out_specs=row,
        out_shape=jax.ShapeDtypeStruct((t, d), F32),
        scratch_shapes=[pltpu.VMEM((tm, f), BF16)],
        compiler_params=_cparams(("parallel",), 48),
        name="ffn",
    )(*args)


def _head_rms(x, ones_ref, gain):
    x2 = x * x
    hi = x2.astype(BF16)
    lo = (x2 - hi.astype(F32)).astype(BF16)
    ss = _dot(hi, ones_ref[...]) + _dot(lo, ones_ref[...])
    return x * lax.rsqrt(ss * (1.0 / ATTN_HEAD_DIM) + NORM_EPS) * gain


def _rope(x, cos, sin_signed):
    n = x.shape[-1]
    lane = lax.broadcasted_iota(jnp.int32, x.shape, 1)
    first = (lane & ROPE_PAIR) == 0
    partner = jnp.where(first, pltpu.roll(x, n - ROPE_PAIR, 1), pltpu.roll(x, ROPE_PAIR, 1))
    return x * cos + partner * sin_signed


def _attn_prep(q, kv, cq_ref, sq_ref, ck_ref, sk_ref, gq_ref, gk_ref, oq_ref, ok_ref, qt_ref, k_ref, vt_ref):
    qt_ref[0] = _rope(_head_rms(q, oq_ref, gq_ref[...]), cq_ref[...], sq_ref[...]).T.astype(BF16)
    k = _rope(_head_rms(kv[:, :ATTN_KV_WIDTH], ok_ref, gk_ref[...]), ck_ref[...], sk_ref[...])
    for g in range(ATTN_KV_HEADS):
        k_ref[0, g] = k[:, g * ATTN_HEAD_DIM:(g + 1) * ATTN_HEAD_DIM].astype(BF16)
    vt = kv[:, ATTN_KV_WIDTH:].T.astype(BF16)
    pad = ATTN_VT_ROWS - ATTN_HEAD_DIM
    ones_row = (lax.broadcasted_iota(jnp.int32, (pad, vt.shape[1]), 0) == 0).astype(F32).astype(BF16)
    for g in range(ATTN_KV_HEADS):
        vt_ref[0, g, :ATTN_HEAD_DIM, :] = vt[g * ATTN_HEAD_DIM:(g + 1) * ATTN_HEAD_DIM]
        vt_ref[0, g, ATTN_HEAD_DIM:, :] = ones_row


def _inproj_body(x_ref, gain_ref, w_ref, cq_ref, sq_ref, ck_ref, sk_ref, gq_ref, gk_ref, oq_ref, ok_ref,
                 o_ref, qt_ref, k_ref, vt_ref, *, chunk):
    h = _rms(x_ref[...], gain_ref[...]).astype(BF16)
    n = o_ref.shape[1]
    q = _dot(h, w_ref[:, n:n + ATTN_WIDTH])
    kv = _dot(h, w_ref[:, n + ATTN_WIDTH:])
    _attn_prep(q, kv, cq_ref, sq_ref, ck_ref, sk_ref, gq_ref, gk_ref, oq_ref, ok_ref, qt_ref, k_ref, vt_ref)
    for c0 in range(0, n, chunk):
        c1 = min(c0 + chunk, n)
        o_ref[:, c0:c1] = _dot(h, w_ref[:, c0:c1])


def _inproj(x2d, seq_len, gain, w, tabs, gq, gk, layer, *, tm=512, chunk=512):
    t, d = x2d.shape
    b, nb = t // seq_len, seq_len // tm
    cq, sq, ck, sk = tabs
    ones_q = jnp.asarray(np.kron(np.eye(ATTN_Q_HEADS), np.ones((ATTN_HEAD_DIM, ATTN_HEAD_DIM))), BF16)
    ones_k = jnp.asarray(np.kron(np.eye(ATTN_KV_HEADS), np.ones((ATTN_HEAD_DIM, ATTN_HEAD_DIM))), BF16)
    tab_q = pl.BlockSpec((tm, ATTN_WIDTH), lambda i: (i % nb, 0))
    tab_k = pl.BlockSpec((tm, ATTN_KV_WIDTH), lambda i: (i % nb, 0))
    return pl.pallas_call(
        functools.partial(_inproj_body, chunk=chunk),
        grid=(t // tm,),
        in_specs=[
            pl.BlockSpec((tm, d), lambda i: (i, 0)), _layer((1, d), layer), _layer(w.shape[1:], layer),
            tab_q, tab_q, tab_k, tab_k,
            _layer((1, ATTN_WIDTH), layer), _layer((1, ATTN_KV_WIDTH), layer),
            _resident((ATTN_WIDTH, ATTN_WIDTH)), _resident((ATTN_KV_WIDTH, ATTN_KV_WIDTH)),
        ],
        out_specs=[
            pl.BlockSpec((tm, PROJ_WIDTH), lambda i: (i, 0)),
            pl.BlockSpec((1, ATTN_WIDTH, tm), lambda i: (i // nb, 0, i % nb)),
            pl.BlockSpec((1, ATTN_KV_HEADS, tm, ATTN_HEAD_DIM), lambda i: (i // nb, 0, i % nb, 0)),
            pl.BlockSpec((1, ATTN_KV_HEADS, ATTN_VT_ROWS, tm), lambda i: (i // nb, 0, 0, i % nb)),
        ],
        out_shape=[
            jax.ShapeDtypeStruct((t, PROJ_WIDTH), F32),
            jax.ShapeDtypeStruct((b, ATTN_WIDTH, seq_len), BF16),
            jax.ShapeDtypeStruct((b, ATTN_KV_HEADS, seq_len, ATTN_HEAD_DIM), BF16),
            jax.ShapeDtypeStruct((b, ATTN_KV_HEADS, ATTN_VT_ROWS, seq_len), BF16),
        ],
        compiler_params=_cparams(("parallel",), 48),
        name="inproj",
    )(x2d, gain, w, cq, sq, ck, sk, gq, gk, ones_q, ones_k)


def _rope_tables(l):
    t = jnp.arange(l, dtype=jnp.int32)
    rows = (t // GRID_W).astype(F32)
    cols = (t % GRID_W).astype(F32)
    half = ATTN_HEAD_DIM // 2
    inv_freq = ROPE_BASE ** (-jnp.arange(half // 2, dtype=F32) * 2.0 / half)
    d = np.arange(ATTN_HEAD_DIM)
    use_rows = jnp.asarray((d // half) == 0)
    freq = inv_freq[jnp.asarray(d % (half // 2))]
    second = jnp.asarray((d % half) >= half // 2)
    ang = jnp.where(use_rows[None, :], rows[:, None], cols[:, None]) * freq[None, :]
    cos = jnp.cos(ang)
    sin = jnp.sin(ang)
    sin = jnp.where(second[None, :], sin, -sin)
    scale = ATTN_HEAD_DIM ** -0.5 * math.log2(math.e)
    return (jnp.tile(cos, (1, ATTN_Q_HEADS)) * scale, jnp.tile(sin, (1, ATTN_Q_HEADS)) * scale,
            jnp.tile(cos, (1, ATTN_KV_HEADS)), jnp.tile(sin, (1, ATTN_KV_HEADS)))


def _attn_body(qt_ref, k_ref, vt_ref, o_ref, *, kb):
    n_kb = k_ref.shape[2] // kb
    items = [(h, j) for h in range(ATTN_GROUP) for j in range(n_kb)]

    def scores(h, j):
        qt = qt_ref[0, h * ATTN_HEAD_DIM:(h + 1) * ATTN_HEAD_DIM, :]
        return _dot(k_ref[0, 0, j * kb:(j + 1) * kb, :], qt)

    outs = []
    st = scores(*items[0])
    run_max = acc = None
    for idx, (h, j) in enumerate(items):
        st_next = scores(*items[idx + 1]) if idx + 1 < len(items) else None
        blk_max = jnp.max(st, axis=0, keepdims=True)
        new_max = blk_max if j == 0 else jnp.maximum(run_max, blk_max)
        p = jnp.exp2(st - new_max)
        blk_acc = _dot(vt_ref[0, 0, :, j * kb:(j + 1) * kb], p.astype(BF16))
        acc = blk_acc if j == 0 else jnp.exp2(run_max - new_max) * acc + blk_acc
        run_max = new_max
        if j == n_kb - 1:
            outs.append((acc[:ATTN_HEAD_DIM] / acc[ATTN_HEAD_DIM:ATTN_HEAD_DIM + 1]).T)
        st = st_next
    o_ref[0] = jnp.concatenate(outs, axis=-1).astype(o_ref.dtype)


def _attention(qt, k, vt, *, tq=1024, kb=512):
    b, _, l = qt.shape
    tq, kb = min(tq, l), min(kb, l)
    assert l % tq == 0 and l % kb == 0
    gw = ATTN_GROUP * ATTN_HEAD_DIM
    return pl.pallas_call(
        functools.partial(_attn_body, kb=kb),
        grid=(b, ATTN_KV_HEADS, l // tq),
        in_specs=[
            pl.BlockSpec((1, gw, tq), lambda bi, g, i: (bi, g, i)),
            pl.BlockSpec((1, 1, l, ATTN_HEAD_DIM), lambda bi, g, i: (bi, g, 0, 0)),
            pl.BlockSpec((1, 1, ATTN_VT_ROWS, l), lambda bi, g, i: (bi, g, 0, 0)),
        ],
        out_specs=pl.BlockSpec((1, tq, gw), lambda bi, g, i: (bi, i, g)),
        out_shape=jax.ShapeDtypeStruct((b, l, ATTN_WIDTH), BF16),
        compiler_params=_cparams(("parallel", "parallel", "parallel"), 48),
        name="attention",
    )(qt, k, vt)


def _gla_body(qf_ref, kf_ref, vf_ref, zf_ref, qb_ref, kb_ref, vb_ref, zb_ref,
              waf_ref, wab_ref, baf_ref, bab_ref, cumf_ref, cumb_ref,
              of_ref, ob_ref, st_ref):
    @pl.when(pl.program_id(1) == 0)
    def _():
        st_ref[...] = jnp.zeros_like(st_ref)

    tb = qf_ref.shape[1]
    n_chunks = tb // GLA_CHUNK
    both = (0, 1)
    heads = [slice(h * GLA_HEAD_DIM, (h + 1) * GLA_HEAD_DIM) for h in range(GLA_HEADS)]
    chunks = [slice(ci * GLA_CHUNK, (ci + 1) * GLA_CHUNK) for ci in range(n_chunks)]
    q_refs, k_refs, v_refs, z_refs = (qf_ref, qb_ref), (kf_ref, kb_ref), (vf_ref, vb_ref), (zf_ref, zb_ref)
    wa_refs, ba_refs, cum_refs, o_refs = (waf_ref, wab_ref), (baf_ref, bab_ref), (cumf_ref, cumb_ref), (of_ref, ob_ref)

    logits = [_dot(z_refs[d][0].astype(BF16), wa_refs[d][...]) + ba_refs[d][...] for d in both]
    log_a = [(jnp.minimum(x, 0.0) - jnp.log(1.0 + jnp.exp(-jnp.abs(x)))) * (1.0 / GLA_TAU) for x in logits]
    hi = [x.astype(BF16) for x in log_a]
    lo = [(x - h.astype(F32)).astype(BF16) for x, h in zip(log_a, hi)]
    bcum = [_dot(cum_refs[d][...], hi[d]) + _dot(cum_refs[d][...], lo[d]) for d in both]
    q_dec, k_dec, k_end, decay, vb, mask = [], [], [], [], [], []
    r = lax.broadcasted_iota(jnp.int32, (tb, tb), 0)
    c = lax.broadcasted_iota(jnp.int32, (tb, tb), 1)
    same = (r // GLA_CHUNK) == (c // GLA_CHUNK)
    for d in both:
        edge = 0 if d else GLA_CHUNK - 1
        btot = jnp.concatenate(
            [jnp.broadcast_to(bcum[d][rows.start + edge:rows.start + edge + 1], (GLA_CHUNK, bcum[d].shape[1]))
             for rows in chunks], axis=0)
        k = k_refs[d][0]
        q_dec.append(((q_refs[d][0] * GLA_HEAD_DIM ** -0.5) * jnp.exp(bcum[d])).astype(BF16))
        k_dec.append((k * jnp.exp(-bcum[d])).astype(BF16))
        k_end.append((k * jnp.exp(btot - bcum[d])).astype(BF16))
        decay.append(jnp.exp(btot))
        vb.append(v_refs[d][0].astype(BF16))
        mask.append(same & ((c >= r) if d else (c <= r)))

    scores = [[jnp.where(mask[d], _dot_nt(q_dec[d][:, hs], k_dec[d][:, hs]), 0.0).astype(BF16) for hs in heads]
              for d in both]
    o_intra = [[_dot(scores[d][h], vb[d][:, hs]) for h, hs in enumerate(heads)] for d in both]
    kv = [[[_dot_tn(vb[d][rows, hs], k_end[d][rows, hs]) for rows in chunks] for hs in heads] for d in both]

    enter = [[[None] * n_chunks for _ in heads] for _ in both]
    for d in both:
        order = range(n_chunks - 1, -1, -1) if d else range(n_chunks)
        for h, hs in enumerate(heads):
            st = st_ref[d, h]
            for ci in order:
                enter[d][h][ci] = st.astype(BF16)
                st = decay[d][chunks[ci].start:chunks[ci].start + 1, hs] * st + kv[d][h][ci]
            st_ref[d, h] = st

    o_inter = [[[_dot_nt(q_dec[d][rows, hs], enter[d][h][ci]) for ci, rows in enumerate(chunks)]
                for h, hs in enumerate(heads)] for d in both]
    for d in both:
        o_refs[d][0] = jnp.concatenate(
            [o_intra[d][h] + jnp.concatenate(o_inter[d][h], axis=0) for h in range(GLA_HEADS)], axis=-1)


def _gla_gate_params(w_alpha, b_alpha):
    depth = w_alpha.shape[0]
    r = GLA_LOWRANK
    zeros = jnp.zeros((depth, LANE, GLA_WIDTH), F32)
    waf = zeros.at[:, :r].set(w_alpha[:, 0]).astype(BF16)
    wab = zeros.at[:, r:2 * r].set(w_alpha[:, 1]).astype(BF16)
    return waf, wab, b_alpha[:, 0].reshape(depth, 1, GLA_WIDTH), b_alpha[:, 1].reshape(depth, 1, GLA_WIDTH)


def _gla(proj, gate_params, layer, *, tb=256):
    b, l, _ = proj.shape
    nblk = l // tb
    idx = np.arange(tb)
    same = (idx[:, None] // GLA_CHUNK) == (idx[None, :] // GLA_CHUNK)
    cum_f = jnp.asarray(same & (idx[None, :] <= idx[:, None]), BF16)
    cum_b = jnp.asarray(same & (idx[None, :] >= idx[:, None]), BF16)

    def col(base, width, rev):
        if rev:
            return pl.BlockSpec((1, tb, width), lambda bi, i: (bi, nblk - 1 - i, base // width))
        return pl.BlockSpec((1, tb, width), lambda bi, i: (bi, i, base // width))

    w = GLA_WIDTH
    in_specs = [col(COL_GQ, w, False), col(COL_GK, w, False), col(COL_GV, w, False), col(COL_Z, LANE, False),
                col(COL_GQ, w, True), col(COL_GK, w, True), col(COL_GV, w, True), col(COL_Z, LANE, True),
                _layer((LANE, w), layer), _layer((LANE, w), layer), _layer((1, w), layer), _layer((1, w), layer),
                _resident((tb, tb)), _resident((tb, tb))]
    return pl.pallas_call(
        _gla_body,
        grid=(b, nblk),
        in_specs=in_specs,
        out_specs=[col(0, w, False), col(0, w, True)],
        out_shape=[jax.ShapeDtypeStruct((b, l, GLA_WIDTH), F32)] * 2,
        scratch_shapes=[pltpu.VMEM((2, GLA_HEADS, GLA_HEAD_DIM, GLA_HEAD_DIM), F32)],
        compiler_params=_cparams(("parallel", "arbitrary"), 32),
        name="gla",
    )(*([proj] * 8), *gate_params, cum_f, cum_b)


def _s5_expand(comb_ref, winc_ref, woc_ref, tz_scr, win_scr, wout_scr):
    tc, lg, h, p = S5_CHUNK, S5_LANE_GROUPS, S5_GROUP_CH, S5_STATE
    nst = lg * p
    row_g = lax.broadcasted_iota(jnp.int32, (LANE, 1), 0) // h
    same = row_g == lax.broadcasted_iota(jnp.int32, (1, LANE), 1) // h
    blocks = [jnp.where(same, jnp.concatenate([comb_ref[lag]] * lg, axis=0), 0.0).astype(BF16)
              for lag in range(2 * tc - 1)]
    for tp in range(tc):
        for t in range(tc):
            tz_scr[tp * LANE:(tp + 1) * LANE, t * LANE:(t + 1) * LANE] = blocks[t - tp + tc - 1]
    same = row_g == (lax.broadcasted_iota(jnp.int32, (1, 4 * nst), 1) % nst) // p
    for tp in range(tc):
        slab = winc_ref[tp * h:(tp + 1) * h, :]
        win_scr[tp * LANE:(tp + 1) * LANE, :] = jnp.where(
            same, jnp.concatenate([slab] * lg, axis=0), 0.0).astype(BF16)
    lane_g = (lax.broadcasted_iota(jnp.int32, (1, tc * LANE), 1) // h) % lg
    for part in range(4):
        tab = woc_ref[part * p:(part + 1) * p, :]
        for g in range(lg):
            r0 = (part * lg + g) * p
            wout_scr[r0:r0 + p, :] = jnp.where(lane_g == g, tab, 0.0).astype(BF16)


def _s5_body(u_ref, comb_ref, winc_ref, woc_ref, a_ref, d_ref, o_ref, tz_scr, win_scr, wout_scr, z_scr, x_scr):
    @pl.when(pl.program_id(1) == 0)
    def _():
        _s5_expand(comb_ref, winc_ref, woc_ref, tz_scr, win_scr, wout_scr)

    l = u_ref.shape[1]
    nc = l // S5_CHUNK
    nst = S5_LANE_GROUPS * S5_STATE
    uk = jnp.concatenate([u_ref[0, pl.ds(t, nc, stride=S5_CHUNK), :] for t in range(S5_CHUNK)], axis=-1)
    ukb = uk.astype(BF16)
    z_scr[...] = _dot(ukb, win_scr[...])
    y_local = _dot(ukb, tz_scr[...]) + uk * d_ref[...]

    a = a_ref[...]
    afr, afi, abr, abi = (a[:, i * nst:(i + 1) * nst] for i in range(4))
    ntile = nc // 8

    def tile_step(i, carry):
        sfr, sfi, sbr, sbi = carry
        rf = pl.multiple_of(i * 8, 8)
        rb = pl.multiple_of((ntile - 1 - i) * 8, 8)
        zf = z_scr[pl.ds(rf, 8), 0:2 * nst]
        zb = z_scr[pl.ds(rb, 8), 2 * nst:4 * nst]
        xfr, xfi, xbr, xbi = [], [], [None] * 8, [None] * 8
        for r in range(8):
            xfr.append(sfr)
            xfi.append(sfi)
            sfr, sfi = (afr * sfr - afi * sfi + zf[r:r + 1, :nst],
                        afr * sfi + afi * sfr + zf[r:r + 1, nst:])
            q = 7 - r
            xbr[q] = sbr
            xbi[q] = sbi
            sbr, sbi = (abr * sbr - abi * sbi + zb[q:q + 1, :nst],
                        abr * sbi + abi * sbr + zb[q:q + 1, nst:])
        x_scr[pl.ds(rf, 8), 0:nst] = jnp.concatenate(xfr, axis=0)
        x_scr[pl.ds(rf, 8), nst:2 * nst] = jnp.concatenate(xfi, axis=0)
        x_scr[pl.ds(rb, 8), 2 * nst:3 * nst] = jnp.concatenate(xbr, axis=0)
        x_scr[pl.ds(rb, 8), 3 * nst:4 * nst] = jnp.concatenate(xbi, axis=0)
        return sfr, sfi, sbr, sbi

    zero = jnp.zeros((1, nst), F32)
    lax.fori_loop(0, ntile, tile_step, (zero, zero, zero, zero), unroll=True)

    y = y_local + _dot(x_scr[...].astype(BF16), wout_scr[...])
    for t in range(S5_CHUNK):
        o_ref[0, pl.ds(t, nc, stride=S5_CHUNK), :] = y[:, t * LANE:(t + 1) * LANE]


def _s5(proj, tables, d_tiled, layer):
    b, l, _ = proj.shape
    comb, winc, woc, a = tables
    nk = S5_WIDTH // LANE
    nc = l // S5_CHUNK
    feat = S5_CHUNK * LANE
    nst4 = 4 * S5_LANE_GROUPS * S5_STATE
    per_block = lambda arr: pl.BlockSpec((None, None) + arr.shape[2:], lambda k, bi: (layer, k, 0, 0))
    return pl.pallas_call(
        _s5_body,
        grid=(nk, b),
        in_specs=[
            pl.BlockSpec((1, l, LANE), lambda k, bi: (bi, 0, COL_S5 // LANE + k)),
            pl.BlockSpec((None,) + comb.shape[1:3] + (LANE,), lambda k, bi: (layer, 0, 0, k)),
            per_block(winc), per_block(woc), per_block(a), per_block(d_tiled),
        ],
        out_specs=pl.BlockSpec((1, l, LANE), lambda k, bi: (bi, 0, k)),
        out_shape=jax.ShapeDtypeStruct((b, l, S5_WIDTH), F32),
        scratch_shapes=[pltpu.VMEM((feat, feat), BF16), pltpu.VMEM((feat, nst4), BF16), pltpu.VMEM((nst4, feat), BF16),
                        pltpu.VMEM((nc, nst4), F32), pltpu.VMEM((nc, nst4), F32)],
        compiler_params=_cparams(("arbitrary", "arbitrary"), 48),
        name="s5",
    )(proj, comb, winc, woc, a, d_tiled)


def _cmul(ar, ai, br, bi):
    return ar * br - ai * bi, ar * bi + ai * br


def _s5_tables(lam_re, lam_im, log_dt, b_re, b_im, c_re, c_im):
    g, p, h, tc, lg = S5_GROUPS, S5_STATE, S5_GROUP_CH, S5_CHUNK, S5_LANE_GROUPS
    nk = g // lg
    dt = jnp.exp(log_dt)[..., None]
    mag = jnp.exp(lam_re * dt)
    lbr, lbi = mag * jnp.cos(lam_im * dt), mag * jnp.sin(lam_im * dt)
    den = lam_re * lam_re + lam_im * lam_im
    nr, ni = lbr - 1.0, lbi
    fr, fi = (nr * lam_re + ni * lam_im) / den, (ni * lam_re - nr * lam_im) / den
    bbr, bbi = _cmul(fr[..., None], fi[..., None], b_re, b_im)
    pwr, pwi = [jnp.ones_like(lbr)], [jnp.zeros_like(lbi)]
    for _ in range(tc):
        nr_, ni_ = _cmul(pwr[-1], pwi[-1], lbr, lbi)
        pwr.append(nr_)
        pwi.append(ni_)
    pwr, pwi = jnp.stack(pwr, 1), jnp.stack(pwi, 1)

    kern = []
    for d in range(2):
        lanes = lambda x: jnp.repeat(x, h, axis=-1)
        cr = c_re[d].transpose(2, 0, 1).reshape(p, 1, 1, g * h)
        ci = c_im[d].transpose(2, 0, 1).reshape(p, 1, 1, g * h)
        pr = lanes(pwr[d, :tc].transpose(2, 0, 1)).reshape(p, tc, 1, g * h)
        pi = lanes(pwi[d, :tc].transpose(2, 0, 1)).reshape(p, tc, 1, g * h)
        br = lanes(bbr[d].transpose(1, 2, 0)).reshape(p, 1, h, g * h)
        bi = lanes(bbi[d].transpose(1, 2, 0)).reshape(p, 1, h, g * h)
        mr, mi = _cmul(cr, ci, pr, pi)
        kern.append(jnp.sum(mr * br - mi * bi, axis=0))
    comb = jnp.concatenate([kern[1][tc - 1:0:-1], (kern[0][0] + kern[1][0])[None], kern[0][1:tc]])

    parts = []
    for d, sel in ((0, np.arange(tc - 1, -1, -1)), (1, np.arange(tc))):
        r_, i_ = _cmul(pwr[d][sel][..., None], pwi[d][sel][..., None], bbr[d][None], bbi[d][None])
        parts += [r_, i_]
    w = jnp.stack([x.transpose(0, 3, 1, 2).reshape(tc, h, nk, lg * p) for x in parts])
    winc = w.transpose(3, 1, 2, 0, 4).reshape(nk, tc * h, 4 * lg * p)

    parts = []
    for d, sel in ((0, np.arange(1, tc + 1)), (1, np.arange(tc, 0, -1))):
        r_, i_ = _cmul(c_re[d][None], c_im[d][None], pwr[d][sel][:, :, None, :], pwi[d][sel][:, :, None, :])
        parts += [r_, -i_]
    wo = jnp.stack([x.reshape(tc, nk, lg, h, p).transpose(1, 4, 0, 2, 3).reshape(nk, p, tc * lg * h) for x in parts], 1)
    woc = wo.reshape(nk, 4 * p, tc * lg * h)

    a = jnp.stack([pwr[0, tc], pwi[0, tc], pwr[1, tc], pwi[1, tc]])
    a = a.reshape(4, nk, lg * p).transpose(1, 0, 2).reshape(nk, 1, 4 * lg * p)
    return comb, winc, woc, a


def _merge_body(x_ref, ys_ref, of_ref, ob_ref, gate_ref, ya_ref, mg_ref, wm_ref, bm_ref, wglu_ref,
                gg_ref, wbs_ref, wbg_ref, wba_ref, wo_ref, o_ref):
    x = x_ref[...]
    d = x.shape[1]
    h = _rms(x, mg_ref[...]).astype(BF16)

    y = jax.nn.gelu(ys_ref[...])
    y_s5 = y * jax.nn.sigmoid(_dot(y.astype(BF16), wglu_ref[...]))

    o = of_ref[...] + ob_ref[...]
    heads = []
    for hh in range(GLA_HEADS):
        oh = o[:, hh * GLA_HEAD_DIM:(hh + 1) * GLA_HEAD_DIM]
        heads.append(oh * lax.rsqrt(jnp.mean(oh * oh, axis=-1, keepdims=True) + NORM_EPS))
    gate = gate_ref[...]
    y_gla = jnp.concatenate(heads, axis=-1) * gg_ref[...] * (gate * jax.nn.sigmoid(gate))

    merged = None
    for i, (val, w_ref) in enumerate(((y_s5.astype(BF16), wbs_ref), (y_gla.astype(BF16), wbg_ref),
                                      (ya_ref[...], wba_ref))):
        g = jax.nn.sigmoid(_dot(h, wm_ref[:, i * d:(i + 1) * d]) + bm_ref[:, i * d:(i + 1) * d])
        term = g * _dot(val, w_ref[...])
        merged = term if merged is None else merged + term
    o_ref[...] = x + _dot(merged.astype(BF16), wo_ref[...])


def _merge(x2d, ys5, o_f, o_b, proj2d, y_attn, mix_gain, w_merge, b_merge, w_glu, gla_gain,
           wb_s5, wb_gla, wb_attn, w_out, layer, *, tm=512):
    t, d = x2d.shape
    per_layer = lambda a: _layer(a.shape[1:], layer)
    row = lambda w: pl.BlockSpec((tm, w), lambda i: (i, 0))
    return pl.pallas_call(
        _merge_body,
        grid=(t // tm,),
        in_specs=[
            row(d), row(S5_WIDTH), row(GLA_WIDTH), row(GLA_WIDTH),
            pl.BlockSpec((tm, GLA_WIDTH), lambda i: (i, COL_GG // GLA_WIDTH)),
            row(ATTN_WIDTH),
            per_layer(mix_gain), per_layer(w_merge), per_layer(b_merge), per_layer(w_glu), per_layer(gla_gain),
            per_layer(wb_s5), per_layer(wb_gla), per_layer(wb_attn), per_layer(w_out),
        ],
        out_specs=row(d),
        out_shape=jax.ShapeDtypeStruct((t, d), F32),
        compiler_params=_cparams(("parallel",), 48),
        name="merge",
    )(x2d, ys5, o_f, o_b, proj2d, y_attn, mix_gain, w_merge, b_merge, w_glu, gla_gain,
      wb_s5, wb_gla, wb_attn, w_out)


def _reorder_w_in(w_in):
    z1 = COL_Z + 2 * GLA_LOWRANK
    pad = jnp.zeros(w_in.shape[:-1] + (PROJ_WIDTH - z1,), w_in.dtype)
    return jnp.concatenate([w_in[..., :z1], pad, w_in[..., z1:]], axis=-1)


def kernel(x, ffn1_norm, ffn1_w_gate, ffn1_w_up, ffn1_w_down, mix_norm, w_in, s5_lambda_re, s5_lambda_im, s5_log_dt, s5_b_re, s5_b_im, s5_c_re, s5_c_im, s5_d, s5_w_glu, gla_w_alpha, gla_b_alpha, gla_norm, attn_q_norm, attn_k_norm, w_branch_s5, w_branch_gla, w_branch_attn, w_merge_gate, b_merge_gate, w_out, ffn2_norm, ffn2_w_gate, ffn2_w_up, ffn2_w_down, final_norm):
    bsz, seq_len, d_model = x.shape
    depth = w_in.shape[0]
    tabs = _rope_tables(seq_len)
    bf = lambda w: w.astype(BF16)
    vec = lambda g: g.reshape(depth, 1, -1)
    ffn1 = (vec(ffn1_norm), bf(ffn1_w_gate), bf(ffn1_w_up), bf(ffn1_w_down))
    ffn2 = (vec(ffn2_norm), bf(ffn2_w_gate), bf(ffn2_w_up), bf(ffn2_w_down))
    mix_gain = vec(mix_norm)
    w_proj = bf(_reorder_w_in(w_in))
    s5_tabs = jax.vmap(_s5_tables)(s5_lambda_re, s5_lambda_im, s5_log_dt, s5_b_re, s5_b_im, s5_c_re, s5_c_im)
    s5_d_tiled = jnp.tile(s5_d.reshape(depth, S5_WIDTH // LANE, 1, LANE), (1, 1, 1, S5_CHUNK))
    gla_gates = _gla_gate_params(gla_w_alpha, gla_b_alpha)
    attn_gq = vec(jnp.tile(attn_q_norm, (1, ATTN_Q_HEADS)))
    attn_gk = vec(jnp.tile(attn_k_norm, (1, ATTN_KV_HEADS)))
    merge_params = (mix_gain, bf(w_merge_gate), vec(b_merge_gate), bf(s5_w_glu), vec(jnp.tile(gla_norm, (1, GLA_HEADS))),
                    bf(w_branch_s5), bf(w_branch_gla), bf(w_branch_attn), bf(w_out))

    flat = lambda a: a.reshape(bsz * seq_len, a.shape[-1])
    x2d = flat(x)
    for i in range(depth):
        x2d = _ffn(x2d, *ffn1, i)

        proj2d, qt, k, vt = _inproj(x2d, seq_len, mix_gain, w_proj, tabs, attn_gq, attn_gk, i)
        proj = proj2d.reshape(bsz, seq_len, PROJ_WIDTH)
        ys5 = _s5(proj, s5_tabs, s5_d_tiled, i)
        o_f, o_b = _gla(proj, gla_gates, i)
        y_attn = _attention(qt, k, vt)
        x2d = _merge(x2d, flat(ys5), flat(o_f), flat(o_b), proj2d, flat(y_attn), *merge_params, i)

        x2d = _ffn(x2d, *ffn2, i, final_norm if i == depth - 1 else None)
    return x2d.reshape(bsz, seq_len, d_model)
```

```python
import functools
import math

import jax
import jax.numpy as jnp
import numpy as np
from jax import lax
from jax.experimental import pallas as pl
from jax.experimental.pallas import tpu as pltpu

F32 = jnp.float32
BF16 = jnp.bfloat16

NORM_EPS = 1e-6
S5_GROUPS = 32
S5_GROUP_CH = 16
S5_STATE = 64
S5_WIDTH = S5_GROUPS * S5_GROUP_CH
S5_CHUNK = 8
S5_LANE_GROUPS = 8
GLA_HEADS = 4
GLA_HEAD_DIM = 128
GLA_WIDTH = GLA_HEADS * GLA_HEAD_DIM
GLA_LOWRANK = 16
GLA_TAU = 16.0
GLA_CHUNK = 64
ATTN_Q_HEADS = 8
ATTN_KV_HEADS = 2
ATTN_HEAD_DIM = 64
ATTN_GROUP = ATTN_Q_HEADS // ATTN_KV_HEADS
ATTN_WIDTH = ATTN_Q_HEADS * ATTN_HEAD_DIM
ATTN_KV_WIDTH = ATTN_KV_HEADS * ATTN_HEAD_DIM
GRID_W = 64
ROPE_BASE = 10000.0
ROPE_PAIR = ATTN_HEAD_DIM // 4
ATTN_VT_ROWS = ATTN_HEAD_DIM + 16
ATTN_BOUND_MAX = 50.0
ATTN_BOUND_SLACK = 1.0 + 2.0 ** -10

LANE = 128
V7X_VMEM_BYTES = 64 * 1024 * 1024

COL_S5 = 0
COL_GQ = 512
COL_GK = 1024
COL_GV = 1536
COL_GG = 2048
COL_Z = 2560
PROJ_WIDTH = 2688


def _cparams(semantics, vmem_mib):
    return pltpu.CompilerParams(
        dimension_semantics=semantics,
        vmem_limit_bytes=min(vmem_mib * 1024 * 1024, V7X_VMEM_BYTES - 4 * 1024 * 1024),
    )


def _resident(shape):
    nd = len(shape)
    return pl.BlockSpec(shape, lambda *_: (0,) * nd, pipeline_mode=pl.Buffered(1))


def _layer(shape, layer):
    nd = len(shape)
    return pl.BlockSpec((None,) + tuple(shape), lambda *_: (layer,) + (0,) * nd, pipeline_mode=pl.Buffered(1))


def _rms(x, gain):
    ms = jnp.mean(x * x, axis=-1, keepdims=True)
    return x * lax.rsqrt(ms + NORM_EPS) * gain


def _dot(a, b):
    return jnp.dot(a, b, preferred_element_type=F32)


def _dot_nt(a, b):
    return lax.dot_general(a, b, (((1,), (1,)), ((), ())), preferred_element_type=F32)


def _dot_tn(a, b):
    return lax.dot_general(a, b, (((0,), (0,)), ((), ())), preferred_element_type=F32)


def _ffn_body(x_ref, gain_ref, wg_ref, wu_ref, wd_ref, *rest, chunk, final):
    if final:
        fg_ref, o_ref, a_ref = rest
    else:
        o_ref, a_ref = rest
    x = x_ref[...]
    h = _rms(x, gain_ref[...]).astype(BF16)
    d_ff = wg_ref.shape[1]
    for c0 in range(0, d_ff, chunk):
        g = _dot(h, wg_ref[:, c0:c0 + chunk])
        u = _dot(h, wu_ref[:, c0:c0 + chunk])
        a_ref[:, c0:c0 + chunk] = (g * jax.nn.sigmoid(g) * u).astype(BF16)
    out = x + 0.5 * _dot(a_ref[...], wd_ref[...])
    if final:
        out = _rms(out, fg_ref[...])
    o_ref[...] = out


def _ffn(x2d, gain, wg, wu, wd, layer, final_gain=None, *, tm=512, chunk=256):
    t, d = x2d.shape
    f = wg.shape[2]
    final = final_gain is not None
    row = pl.BlockSpec((tm, d), lambda i: (i, 0))
    in_specs = [row, _layer((1, d), layer), _layer((d, f), layer), _layer((d, f), layer), _layer((f, d), layer)]
    args = [x2d, gain, wg, wu, wd]
    if final:
        in_specs.append(_resident((1, d)))
        args.append(final_gain.reshape(1, d))
    return pl.pallas_call(
        functools.partial(_ffn_body, chunk=chunk, final=final),
        grid=(t // tm,),
        in_specs=in_specs,
        out_specs=row,
        out_shape=jax.ShapeDtypeStruct((t, d), F32),
        scratch_shapes=[pltpu.VMEM((tm, f), BF16)],
        compiler_params=_cparams(("parallel",), 48),
        name="ffn",
    )(*args)


def _head_rms(x, ones_ref, gain):
    x2 = x * x
    hi = x2.astype(BF16)
    lo = (x2 - hi.astype(F32)).astype(BF16)
    ss = _dot(hi, ones_ref[...]) + _dot(lo, ones_ref[...])
    return x * lax.rsqrt(ss * (1.0 / ATTN_HEAD_DIM) + NORM_EPS) * gain


def _rope(x, cos, sin_signed):
    n = x.shape[-1]
    lane = lax.broadcasted_iota(jnp.int32, x.shape, 1)
    first = (lane & ROPE_PAIR) == 0
    partner = jnp.where(first, pltpu.roll(x, n - ROPE_PAIR, 1), pltpu.roll(x, ROPE_PAIR, 1))
    return x * cos + partner * sin_signed


def _attn_prep(q, kv, cq_ref, sq_ref, ck_ref, sk_ref, gq_ref, gk_ref, oq_ref, ok_ref, qt_ref, k_ref, vt_ref):
    qt_ref[0] = _rope(_head_rms(q, oq_ref, gq_ref[...]), cq_ref[...], sq_ref[...]).T.astype(BF16)
    k = _rope(_head_rms(kv[:, :ATTN_KV_WIDTH], ok_ref, gk_ref[...]), ck_ref[...], sk_ref[...])
    for g in range(ATTN_KV_HEADS):
        k_ref[0, g] = k[:, g * ATTN_HEAD_DIM:(g + 1) * ATTN_HEAD_DIM].astype(BF16)
    vt = kv[:, ATTN_KV_WIDTH:].T.astype(BF16)
    pad = ATTN_VT_ROWS - ATTN_HEAD_DIM
    ones_row = (lax.broadcasted_iota(jnp.int32, (pad, vt.shape[1]), 0) == 0).astype(F32).astype(BF16)
    for g in range(ATTN_KV_HEADS):
        vt_ref[0, g, :ATTN_HEAD_DIM, :] = vt[g * ATTN_HEAD_DIM:(g + 1) * ATTN_HEAD_DIM]
        vt_ref[0, g, ATTN_HEAD_DIM:, :] = ones_row


def _inproj_body(x_ref, gain_ref, w_ref, cq_ref, sq_ref, ck_ref, sk_ref, gq_ref, gk_ref, oq_ref, ok_ref,
                 o_ref, qt_ref, k_ref, vt_ref, *, chunk):
    h = _rms(x_ref[...], gain_ref[...]).astype(BF16)
    n = o_ref.shape[1]
    q = _dot(h, w_ref[:, n:n + ATTN_WIDTH])
    kv = _dot(h, w_ref[:, n + ATTN_WIDTH:])
    _attn_prep(q, kv, cq_ref, sq_ref, ck_ref, sk_ref, gq_ref, gk_ref, oq_ref, ok_ref, qt_ref, k_ref, vt_ref)
    for c0 in range(0, n, chunk):
        c1 = min(c0 + chunk, n)
        o_ref[:, c0:c1] = _dot(h, w_ref[:, c0:c1])


def _inproj(x2d, seq_len, gain, w, tabs, gq, gk, layer, *, tm=512, chunk=512):
    t, d = x2d.shape
    b, nb = t // seq_len, seq_len // tm
    cq, sq, ck, sk = tabs
    ones_q = jnp.asarray(np.kron(np.eye(ATTN_Q_HEADS), np.ones((ATTN_HEAD_DIM, ATTN_HEAD_DIM))), BF16)
    ones_k = jnp.asarray(np.kron(np.eye(ATTN_KV_HEADS), np.ones((ATTN_HEAD_DIM, ATTN_HEAD_DIM))), BF16)
    tab_q = pl.BlockSpec((tm, ATTN_WIDTH), lambda i: (i % nb, 0))
    tab_k = pl.BlockSpec((tm, ATTN_KV_WIDTH), lambda i: (i % nb, 0))
    return pl.pallas_call(
        functools.partial(_inproj_body, chunk=chunk),
        grid=(t // tm,),
        in_specs=[
            pl.BlockSpec((tm, d), lambda i: (i, 0)), _layer((1, d), layer), _layer(w.shape[1:], layer),
            tab_q, tab_q, tab_k, tab_k,
            _layer((1, ATTN_WIDTH), layer), _layer((1, ATTN_KV_WIDTH), layer),
            _resident((ATTN_WIDTH, ATTN_WIDTH)), _resident((ATTN_KV_WIDTH, ATTN_KV_WIDTH)),
        ],
        out_specs=[
            pl.BlockSpec((tm, PROJ_WIDTH), lambda i: (i, 0)),
            pl.BlockSpec((1, ATTN_WIDTH, tm), lambda i: (i // nb, 0, i % nb)),
            pl.BlockSpec((1, ATTN_KV_HEADS, tm, ATTN_HEAD_DIM), lambda i: (i // nb, 0, i % nb, 0)),
            pl.BlockSpec((1, ATTN_KV_HEADS, ATTN_VT_ROWS, tm), lambda i: (i // nb, 0, 0, i % nb)),
        ],
        out_shape=[
            jax.ShapeDtypeStruct((t, PROJ_WIDTH), F32),
            jax.ShapeDtypeStruct((b, ATTN_WIDTH, seq_len), BF16),
            jax.ShapeDtypeStruct((b, ATTN_KV_HEADS, seq_len, ATTN_HEAD_DIM), BF16),
            jax.ShapeDtypeStruct((b, ATTN_KV_HEADS, ATTN_VT_ROWS, seq_len), BF16),
        ],
        compiler_params=_cparams(("parallel",), 48),
        name="inproj",
    )(x2d, gain, w, cq, sq, ck, sk, gq, gk, ones_q, ones_k)


def _rope_tables(l):
    t = jnp.arange(l, dtype=jnp.int32)
    rows = (t // GRID_W).astype(F32)
    cols = (t % GRID_W).astype(F32)
    half = ATTN_HEAD_DIM // 2
    inv_freq = ROPE_BASE ** (-jnp.arange(half // 2, dtype=F32) * 2.0 / half)
    d = np.arange(ATTN_HEAD_DIM)
    use_rows = jnp.asarray((d // half) == 0)
    freq = inv_freq[jnp.asarray(d % (half // 2))]
    second = jnp.asarray((d % half) >= half // 2)
    ang = jnp.where(use_rows[None, :], rows[:, None], cols[:, None]) * freq[None, :]
    cos = jnp.cos(ang)
    sin = jnp.sin(ang)
    sin = jnp.where(second[None, :], sin, -sin)
    scale = ATTN_HEAD_DIM ** -0.5 * math.log2(math.e)
    return (jnp.tile(cos, (1, ATTN_Q_HEADS)) * scale, jnp.tile(sin, (1, ATTN_Q_HEADS)) * scale,
            jnp.tile(cos, (1, ATTN_KV_HEADS)), jnp.tile(sin, (1, ATTN_KV_HEADS)))


def _attn_body(qt_ref, k_ref, vt_ref, o_ref, kmax_scr, *, kb):
    n_kb = k_ref.shape[2] // kb
    items = [(h, j) for h in range(ATTN_GROUP) for j in range(n_kb)]
    heads = [slice(h * ATTN_HEAD_DIM, (h + 1) * ATTN_HEAD_DIM) for h in range(ATTN_GROUP)]

    def scores(h, j):
        return _dot(k_ref[0, 0, j * kb:(j + 1) * kb, :], qt_ref[0, heads[h], :])

    def values(j, p):
        return _dot(vt_ref[0, 0, :, j * kb:(j + 1) * kb], p.astype(BF16))

    def finish(accs):
        outs = [(a[:ATTN_HEAD_DIM] / a[ATTN_HEAD_DIM:ATTN_HEAD_DIM + 1]).T for a in accs]
        o_ref[0] = jnp.concatenate(outs, axis=-1).astype(o_ref.dtype)

    @pl.when(pl.program_id(2) == 0)
    def _():
        kf = k_ref[0, 0].astype(F32)
        k_max2 = jnp.max(jnp.sum(kf * kf, axis=-1, keepdims=True), axis=0, keepdims=True)
        kmax_scr[...] = jnp.broadcast_to(k_max2, kmax_scr.shape)

    k_max2 = kmax_scr[0:1, 0:1]
    qf = qt_ref[0].astype(F32)
    bounds = [jnp.sqrt(jnp.sum(qf[hs] * qf[hs], axis=0, keepdims=True) * k_max2) * ATTN_BOUND_SLACK
              for hs in heads]
    worst = bounds[0]
    for b in bounds[1:]:
        worst = jnp.maximum(worst, b)
    bounded = jnp.max(worst) <= ATTN_BOUND_MAX

    @pl.when(bounded)
    def _():
        accs = [None] * ATTN_GROUP
        st = scores(*items[0])
        for idx, (h, j) in enumerate(items):
            st_next = scores(*items[idx + 1]) if idx + 1 < len(items) else None
            blk = values(j, jnp.exp2(st - bounds[h]))
            accs[h] = blk if j == 0 else accs[h] + blk
            st = st_next
        finish(accs)

    @pl.when(jnp.logical_not(bounded))
    def _():
        accs = [None] * ATTN_GROUP
        st = scores(*items[0])
        run_max = None
        for idx, (h, j) in enumerate(items):
            st_next = scores(*items[idx + 1]) if idx + 1 < len(items) else None
            blk_max = jnp.max(st, axis=0, keepdims=True)
            new_max = blk_max if j == 0 else jnp.maximum(run_max, blk_max)
            blk = values(j, jnp.exp2(st - new_max))
            accs[h] = blk if j == 0 else jnp.exp2(run_max - new_max) * accs[h] + blk
            run_max = new_max
            st = st_next
        finish(accs)


def _attention(qt, k, vt, *, tq=1024, kb=512):
    b, _, l = qt.shape
    tq, kb = min(tq, l), min(kb, l)
    assert l % tq == 0 and l % kb == 0
    gw = ATTN_GROUP * ATTN_HEAD_DIM
    return pl.pallas_call(
        functools.partial(_attn_body, kb=kb),
        grid=(b, ATTN_KV_HEADS, l // tq),
        in_specs=[
            pl.BlockSpec((1, gw, tq), lambda bi, g, i: (bi, g, i)),
            pl.BlockSpec((1, 1, l, ATTN_HEAD_DIM), lambda bi, g, i: (bi, g, 0, 0)),
            pl.BlockSpec((1, 1, ATTN_VT_ROWS, l), lambda bi, g, i: (bi, g, 0, 0)),
        ],
        out_specs=pl.BlockSpec((1, tq, gw), lambda bi, g, i: (bi, i, g)),
        out_shape=jax.ShapeDtypeStruct((b, l, ATTN_WIDTH), BF16),
        scratch_shapes=[pltpu.VMEM((8, LANE), F32)],
        compiler_params=_cparams(("parallel", "parallel", "arbitrary"), 48),
        name="attention",
    )(qt, k, vt)


def _gla_body(qf_ref, kf_ref, vf_ref, zf_ref, qb_ref, kb_ref, vb_ref, zb_ref,
              waf_ref, wab_ref, baf_ref, bab_ref, cumf_ref, cumb_ref,
              of_ref, ob_ref, st_ref):
    @pl.when(pl.program_id(1) == 0)
    def _():
        st_ref[...] = jnp.zeros_like(st_ref)

    tb = qf_ref.shape[1]
    n_chunks = tb // GLA_CHUNK
    both = (0, 1)
    heads = [slice(h * GLA_HEAD_DIM, (h + 1) * GLA_HEAD_DIM) for h in range(GLA_HEADS)]
    chunks = [slice(ci * GLA_CHUNK, (ci + 1) * GLA_CHUNK) for ci in range(n_chunks)]
    q_refs, k_refs, v_refs, z_refs = (qf_ref, qb_ref), (kf_ref, kb_ref), (vf_ref, vb_ref), (zf_ref, zb_ref)
    wa_refs, ba_refs, cum_refs, o_refs = (waf_ref, wab_ref), (baf_ref, bab_ref), (cumf_ref, cumb_ref), (of_ref, ob_ref)

    logits = [_dot(z_refs[d][0].astype(BF16), wa_refs[d][...]) + ba_refs[d][...] for d in both]
    log_a = [(jnp.minimum(x, 0.0) - jnp.log(1.0 + jnp.exp(-jnp.abs(x)))) * (1.0 / GLA_TAU) for x in logits]
    hi = [x.astype(BF16) for x in log_a]
    lo = [(x - h.astype(F32)).astype(BF16) for x, h in zip(log_a, hi)]
    bcum = [_dot(cum_refs[d][...], hi[d]) + _dot(cum_refs[d][...], lo[d]) for d in both]
    q_dec, k_dec, k_end, decay, vb, mask = [], [], [], [], [], []
    r = lax.broadcasted_iota(jnp.int32, (tb, tb), 0)
    c = lax.broadcasted_iota(jnp.int32, (tb, tb), 1)
    same = (r // GLA_CHUNK) == (c // GLA_CHUNK)
    for d in both:
        edge = 0 if d else GLA_CHUNK - 1
        btot = jnp.concatenate(
            [jnp.broadcast_to(bcum[d][rows.start + edge:rows.start + edge + 1], (GLA_CHUNK, bcum[d].shape[1]))
             for rows in chunks], axis=0)
        k = k_refs[d][0]
        q_dec.append(((q_refs[d][0] * GLA_HEAD_DIM ** -0.5) * jnp.exp(bcum[d])).astype(BF16))
        k_dec.append((k * jnp.exp(-bcum[d])).astype(BF16))
        k_end.append((k * jnp.exp(btot - bcum[d])).astype(BF16))
        decay.append(jnp.exp(btot))
        vb.append(v_refs[d][0].astype(BF16))
        mask.append(same & ((c >= r) if d else (c <= r)))

    scores = [[jnp.where(mask[d], _dot_nt(q_dec[d][:, hs], k_dec[d][:, hs]), 0.0).astype(BF16) for hs in heads]
              for d in both]
    o_intra = [[_dot(scores[d][h], vb[d][:, hs]) for h, hs in enumerate(heads)] for d in both]
    kv = [[[_dot_tn(vb[d][rows, hs], k_end[d][rows, hs]) for rows in chunks] for hs in heads] for d in both]

    enter = [[[None] * n_chunks for _ in heads] for _ in both]
    for d in both:
        order = range(n_chunks - 1, -1, -1) if d else range(n_chunks)
        for h, hs in enumerate(heads):
            st = st_ref[d, h]
            for ci in order:
                enter[d][h][ci] = st.astype(BF16)
                st = decay[d][chunks[ci].start:chunks[ci].start + 1, hs] * st + kv[d][h][ci]
            st_ref[d, h] = st

    o_inter = [[[_dot_nt(q_dec[d][rows, hs], enter[d][h][ci]) for ci, rows in enumerate(chunks)]
                for h, hs in enumerate(heads)] for d in both]
    for d in both:
        o_refs[d][0] = jnp.concatenate(
            [o_intra[d][h] + jnp.concatenate(o_inter[d][h], axis=0) for h in range(GLA_HEADS)], axis=-1)


def _gla_gate_params(w_alpha, b_alpha):
    depth = w_alpha.shape[0]
    r = GLA_LOWRANK
    zeros = jnp.zeros((depth, LANE, GLA_WIDTH), F32)
    waf = zeros.at[:, :r].set(w_alpha[:, 0]).astype(BF16)
    wab = zeros.at[:, r:2 * r].set(w_alpha[:, 1]).astype(BF16)
    return waf, wab, b_alpha[:, 0].reshape(depth, 1, GLA_WIDTH), b_alpha[:, 1].reshape(depth, 1, GLA_WIDTH)


def _gla(proj, gate_params, layer, *, tb=256):
    b, l, _ = proj.shape
    nblk = l // tb
    idx = np.arange(tb)
    same = (idx[:, None] // GLA_CHUNK) == (idx[None, :] // GLA_CHUNK)
    cum_f = jnp.asarray(same & (idx[None, :] <= idx[:, None]), BF16)
    cum_b = jnp.asarray(same & (idx[None, :] >= idx[:, None]), BF16)

    def col(base, width, rev):
        if rev:
            return pl.BlockSpec((1, tb, width), lambda bi, i: (bi, nblk - 1 - i, base // width))
        return pl.BlockSpec((1, tb, width), lambda bi, i: (bi, i, base // width))

    w = GLA_WIDTH
    in_specs = [col(COL_GQ, w, False), col(COL_GK, w, False), col(COL_GV, w, False), col(COL_Z, LANE, False),
                col(COL_GQ, w, True), col(COL_GK, w, True), col(COL_GV, w, True), col(COL_Z, LANE, True),
                _layer((LANE, w), layer), _layer((LANE, w), layer), _layer((1, w), layer), _layer((1, w), layer),
                _resident((tb, tb)), _resident((tb, tb))]
    return pl.pallas_call(
        _gla_body,
        grid=(b, nblk),
        in_specs=in_specs,
        out_specs=[col(0, w, False), col(0, w, True)],
        out_shape=[jax.ShapeDtypeStruct((b, l, GLA_WIDTH), F32)] * 2,
        scratch_shapes=[pltpu.VMEM((2, GLA_HEADS, GLA_HEAD_DIM, GLA_HEAD_DIM), F32)],
        compiler_params=_cparams(("parallel", "arbitrary"), 32),
        name="gla",
    )(*([proj] * 8), *gate_params, cum_f, cum_b)


def _s5_expand(comb_ref, winc_ref, woc_ref, tz_scr, win_scr, wout_scr):
    tc, lg, h, p = S5_CHUNK, S5_LANE_GROUPS, S5_GROUP_CH, S5_STATE
    nst = lg * p
    row_g = lax.broadcasted_iota(jnp.int32, (LANE, 1), 0) // h
    same = row_g == lax.broadcasted_iota(jnp.int32, (1, LANE), 1) // h
    blocks = [jnp.where(same, jnp.concatenate([comb_ref[lag]] * lg, axis=0), 0.0).astype(BF16)
              for lag in range(2 * tc - 1)]
    for tp in range(tc):
        for t in range(tc):
            tz_scr[tp * LANE:(tp + 1) * LANE, t * LANE:(t + 1) * LANE] = blocks[t - tp + tc - 1]
    same = row_g == (lax.broadcasted_iota(jnp.int32, (1, 4 * nst), 1) % nst) // p
    for tp in range(tc):
        slab = winc_ref[tp * h:(tp + 1) * h, :]
        win_scr[tp * LANE:(tp + 1) * LANE, :] = jnp.where(
            same, jnp.concatenate([slab] * lg, axis=0), 0.0).astype(BF16)
    lane_g = (lax.broadcasted_iota(jnp.int32, (1, tc * LANE), 1) // h) % lg
    for part in range(4):
        tab = woc_ref[part * p:(part + 1) * p, :]
        for g in range(lg):
            r0 = (part * lg + g) * p
            wout_scr[r0:r0 + p, :] = jnp.where(lane_g == g, tab, 0.0).astype(BF16)


def _s5_body(u_ref, comb_ref, winc_ref, woc_ref, a_ref, d_ref, o_ref, tz_scr, win_scr, wout_scr, z_scr, x_scr):
    @pl.when(pl.program_id(1) == 0)
    def _():
        _s5_expand(comb_ref, winc_ref, woc_ref, tz_scr, win_scr, wout_scr)

    l = u_ref.shape[1]
    nc = l // S5_CHUNK
    nst = S5_LANE_GROUPS * S5_STATE
    uk = jnp.concatenate([u_ref[0, pl.ds(t, nc, stride=S5_CHUNK), :] for t in range(S5_CHUNK)], axis=-1)
    ukb = uk.astype(BF16)
    z_scr[...] = _dot(ukb, win_scr[...])
    y_local = _dot(ukb, tz_scr[...]) + uk * d_ref[...]

    a = a_ref[...]
    afr, afi, abr, abi = (a[:, i * nst:(i + 1) * nst] for i in range(4))
    ntile = nc // 8

    def tile_step(i, carry):
        sfr, sfi, sbr, sbi = carry
        rf = pl.multiple_of(i * 8, 8)
        rb = pl.multiple_of((ntile - 1 - i) * 8, 8)
        zf = z_scr[pl.ds(rf, 8), 0:2 * nst]
        zb = z_scr[pl.ds(rb, 8), 2 * nst:4 * nst]
        xfr, xfi, xbr, xbi = [], [], [None] * 8, [None] * 8
        for r in range(8):
            xfr.append(sfr)
            xfi.append(sfi)
            sfr, sfi = (afr * sfr - afi * sfi + zf[r:r + 1, :nst],
                        afr * sfi + afi * sfr + zf[r:r + 1, nst:])
            q = 7 - r
            xbr[q] = sbr
            xbi[q] = sbi
            sbr, sbi = (abr * sbr - abi * sbi + zb[q:q + 1, :nst],
                        abr * sbi + abi * sbr + zb[q:q + 1, nst:])
        x_scr[pl.ds(rf, 8), 0:nst] = jnp.concatenate(xfr, axis=0)
        x_scr[pl.ds(rf, 8), nst:2 * nst] = jnp.concatenate(xfi, axis=0)
        x_scr[pl.ds(rb, 8), 2 * nst:3 * nst] = jnp.concatenate(xbr, axis=0)
        x_scr[pl.ds(rb, 8), 3 * nst:4 * nst] = jnp.concatenate(xbi, axis=0)
        return sfr, sfi, sbr, sbi

    zero = jnp.zeros((1, nst), F32)
    lax.fori_loop(0, ntile, tile_step, (zero, zero, zero, zero), unroll=True)

    y = y_local + _dot(x_scr[...].astype(BF16), wout_scr[...])
    for t in range(S5_CHUNK):
        o_ref[0, pl.ds(t, nc, stride=S5_CHUNK), :] = y[:, t * LANE:(t + 1) * LANE]


def _s5(proj, tables, d_tiled, layer):
    b, l, _ = proj.shape
    comb, winc, woc, a = tables
    nk = S5_WIDTH // LANE
    nc = l // S5_CHUNK
    feat = S5_CHUNK * LANE
    nst4 = 4 * S5_LANE_GROUPS * S5_STATE
    per_block = lambda arr: pl.BlockSpec((None, None) + arr.shape[2:], lambda k, bi: (layer, k, 0, 0))
    return pl.pallas_call(
        _s5_body,
        grid=(nk, b),
        in_specs=[
            pl.BlockSpec((1, l, LANE), lambda k, bi: (bi, 0, COL_S5 // LANE + k)),
            pl.BlockSpec((None,) + comb.shape[1:3] + (LANE,), lambda k, bi: (layer, 0, 0, k)),
            per_block(winc), per_block(woc), per_block(a), per_block(d_tiled),
        ],
        out_specs=pl.BlockSpec((1, l, LANE), lambda k, bi: (bi, 0, k)),
        out_shape=jax.ShapeDtypeStruct((b, l, S5_WIDTH), F32),
        scratch_shapes=[pltpu.VMEM((feat, feat), BF16), pltpu.VMEM((feat, nst4), BF16), pltpu.VMEM((nst4, feat), BF16),
                        pltpu.VMEM((nc, nst4), F32), pltpu.VMEM((nc, nst4), F32)],
        compiler_params=_cparams(("arbitrary", "arbitrary"), 48),
        name="s5",
    )(proj, comb, winc, woc, a, d_tiled)


def _cmul(ar, ai, br, bi):
    return ar * br - ai * bi, ar * bi + ai * br


def _s5_tables(lam_re, lam_im, log_dt, b_re, b_im, c_re, c_im):
    g, p, h, tc, lg = S5_GROUPS, S5_STATE, S5_GROUP_CH, S5_CHUNK, S5_LANE_GROUPS
    nk = g // lg
    dt = jnp.exp(log_dt)[..., None]
    mag = jnp.exp(lam_re * dt)
    lbr, lbi = mag * jnp.cos(lam_im * dt), mag * jnp.sin(lam_im * dt)
    den = lam_re * lam_re + lam_im * lam_im
    nr, ni = lbr - 1.0, lbi
    fr, fi = (nr * lam_re + ni * lam_im) / den, (ni * lam_re - nr * lam_im) / den
    bbr, bbi = _cmul(fr[..., None], fi[..., None], b_re, b_im)
    pwr, pwi = [jnp.ones_like(lbr)], [jnp.zeros_like(lbi)]
    for _ in range(tc):
        nr_, ni_ = _cmul(pwr[-1], pwi[-1], lbr, lbi)
        pwr.append(nr_)
        pwi.append(ni_)
    pwr, pwi = jnp.stack(pwr, 1), jnp.stack(pwi, 1)

    kern = []
    for d in range(2):
        lanes = lambda x: jnp.repeat(x, h, axis=-1)
        cr = c_re[d].transpose(2, 0, 1).reshape(p, 1, 1, g * h)
        ci = c_im[d].transpose(2, 0, 1).reshape(p, 1, 1, g * h)
        pr = lanes(pwr[d, :tc].transpose(2, 0, 1)).reshape(p, tc, 1, g * h)
        pi = lanes(pwi[d, :tc].transpose(2, 0, 1)).reshape(p, tc, 1, g * h)
        br = lanes(bbr[d].transpose(1, 2, 0)).reshape(p, 1, h, g * h)
        bi = lanes(bbi[d].transpose(1, 2, 0)).reshape(p, 1, h, g * h)
        mr, mi = _cmul(cr, ci, pr, pi)
        kern.append(jnp.sum(mr * br - mi * bi, axis=0))
    comb = jnp.concatenate([kern[1][tc - 1:0:-1], (kern[0][0] + kern[1][0])[None], kern[0][1:tc]])

    parts = []
    for d, sel in ((0, np.arange(tc - 1, -1, -1)), (1, np.arange(tc))):
        r_, i_ = _cmul(pwr[d][sel][..., None], pwi[d][sel][..., None], bbr[d][None], bbi[d][None])
        parts += [r_, i_]
    w = jnp.stack([x.transpose(0, 3, 1, 2).reshape(tc, h, nk, lg * p) for x in parts])
    winc = w.transpose(3, 1, 2, 0, 4).reshape(nk, tc * h, 4 * lg * p)

    parts = []
    for d, sel in ((0, np.arange(1, tc + 1)), (1, np.arange(tc, 0, -1))):
        r_, i_ = _cmul(c_re[d][None], c_im[d][None], pwr[d][sel][:, :, None, :], pwi[d][sel][:, :, None, :])
        parts += [r_, -i_]
    wo = jnp.stack([x.reshape(tc, nk, lg, h, p).transpose(1, 4, 0, 2, 3).reshape(nk, p, tc * lg * h) for x in parts], 1)
    woc = wo.reshape(nk, 4 * p, tc * lg * h)

    a = jnp.stack([pwr[0, tc], pwi[0, tc], pwr[1, tc], pwi[1, tc]])
    a = a.reshape(4, nk, lg * p).transpose(1, 0, 2).reshape(nk, 1, 4 * lg * p)
    return comb, winc, woc, a


def _merge_body(x_ref, ys_ref, of_ref, ob_ref, gate_ref, ya_ref, mg_ref, wm_ref, bm_ref, wglu_ref,
                gg_ref, wbs_ref, wbg_ref, wba_ref, wo_ref, o_ref):
    x = x_ref[...]
    d = x.shape[1]
    h = _rms(x, mg_ref[...]).astype(BF16)

    y = jax.nn.gelu(ys_ref[...])
    y_s5 = y * jax.nn.sigmoid(_dot(y.astype(BF16), wglu_ref[...]))

    o = of_ref[...] + ob_ref[...]
    heads = []
    for hh in range(GLA_HEADS):
        oh = o[:, hh * GLA_HEAD_DIM:(hh + 1) * GLA_HEAD_DIM]
        heads.append(oh * lax.rsqrt(jnp.mean(oh * oh, axis=-1, keepdims=True) + NORM_EPS))
    gate = gate_ref[...]
    y_gla = jnp.concatenate(heads, axis=-1) * gg_ref[...] * (gate * jax.nn.sigmoid(gate))

    merged = None
    for i, (val, w_ref) in enumerate(((y_s5.astype(BF16), wbs_ref), (y_gla.astype(BF16), wbg_ref),
                                      (ya_ref[...], wba_ref))):
        g = jax.nn.sigmoid(_dot(h, wm_ref[:, i * d:(i + 1) * d]) + bm_ref[:, i * d:(i + 1) * d])
        term = g * _dot(val, w_ref[...])
        merged = term if merged is None else merged + term
    o_ref[...] = x + _dot(merged.astype(BF16), wo_ref[...])


def _merge(x2d, ys5, o_f, o_b, proj2d, y_attn, mix_gain, w_merge, b_merge, w_glu, gla_gain,
           wb_s5, wb_gla, wb_attn, w_out, layer, *, tm=512):
    t, d = x2d.shape
    per_layer = lambda a: _layer(a.shape[1:], layer)
    row = lambda w: pl.BlockSpec((tm, w), lambda i: (i, 0))
    return pl.pallas_call(
        _merge_body,
        grid=(t // tm,),
        in_specs=[
            row(d), row(S5_WIDTH), row(GLA_WIDTH), row(GLA_WIDTH),
            pl.BlockSpec((tm, GLA_WIDTH), lambda i: (i, COL_GG // GLA_WIDTH)),
            row(ATTN_WIDTH),
            per_layer(mix_gain), per_layer(w_merge), per_layer(b_merge), per_layer(w_glu), per_layer(gla_gain),
            per_layer(wb_s5), per_layer(wb_gla), per_layer(wb_attn), per_layer(w_out),
        ],
        out_specs=row(d),
        out_shape=jax.ShapeDtypeStruct((t, d), F32),
        compiler_params=_cparams(("parallel",), 48),
        name="merge",
    )(x2d, ys5, o_f, o_b, proj2d, y_attn, mix_gain, w_merge, b_merge, w_glu, gla_gain,
      wb_s5, wb_gla, wb_attn, w_out)


def _reorder_w_in(w_in):
    z1 = COL_Z + 2 * GLA_LOWRANK
    pad = jnp.zeros(w_in.shape[:-1] + (PROJ_WIDTH - z1,), w_in.dtype)
    return jnp.concatenate([w_in[..., :z1], pad, w_in[..., z1:]], axis=-1)


def kernel(x, ffn1_norm, ffn1_w_gate, ffn1_w_up, ffn1_w_down, mix_norm, w_in, s5_lambda_re, s5_lambda_im, s5_log_dt, s5_b_re, s5_b_im, s5_c_re, s5_c_im, s5_d, s5_w_glu, gla_w_alpha, gla_b_alpha, gla_norm, attn_q_norm, attn_k_norm, w_branch_s5, w_branch_gla, w_branch_attn, w_merge_gate, b_merge_gate, w_out, ffn2_norm, ffn2_w_gate, ffn2_w_up, ffn2_w_down, final_norm):
    bsz, seq_len, d_model = x.shape
    depth = w_in.shape[0]
    tabs = _rope_tables(seq_len)
    bf = lambda w: w.astype(BF16)
    vec = lambda g: g.reshape(depth, 1, -1)
    ffn1 = (vec(ffn1_norm), bf(ffn1_w_gate), bf(ffn1_w_up), bf(ffn1_w_down))
    ffn2 = (vec(ffn2_norm), bf(ffn2_w_gate), bf(ffn2_w_up), bf(ffn2_w_down))
    mix_gain = vec(mix_norm)
    w_proj = bf(_reorder_w_in(w_in))
    s5_tabs = jax.vmap(_s5_tables)(s5_lambda_re, s5_lambda_im, s5_log_dt, s5_b_re, s5_b_im, s5_c_re, s5_c_im)
    s5_d_tiled = jnp.tile(s5_d.reshape(depth, S5_WIDTH // LANE, 1, LANE), (1, 1, 1, S5_CHUNK))
    gla_gates = _gla_gate_params(gla_w_alpha, gla_b_alpha)
    attn_gq = vec(jnp.tile(attn_q_norm, (1, ATTN_Q_HEADS)))
    attn_gk = vec(jnp.tile(attn_k_norm, (1, ATTN_KV_HEADS)))
    merge_params = (mix_gain, bf(w_merge_gate), vec(b_merge_gate), bf(s5_w_glu), vec(jnp.tile(gla_norm, (1, GLA_HEADS))),
                    bf(w_branch_s5), bf(w_branch_gla), bf(w_branch_attn), bf(w_out))

    flat = lambda a: a.reshape(bsz * seq_len, a.shape[-1])
    x2d = flat(x)
    for i in range(depth):
        x2d = _ffn(x2d, *ffn1, i)

        proj2d, qt, k, vt = _inproj(x2d, seq_len, mix_gain, w_proj, tabs, attn_gq, attn_gk, i)
        proj = proj2d.reshape(bsz, seq_len, PROJ_WIDTH)
        ys5 = _s5(proj, s5_tabs, s5_d_tiled, i)
        o_f, o_b = _gla(proj, gla_gates, i)
        y_attn = _attention(qt, k, vt)
        x2d = _merge(x2d, flat(ys5), flat(o_f), flat(o_b), proj2d, flat(y_attn), *merge_params, i)

        x2d = _ffn(x2d, *ffn2, i, final_norm if i == depth - 1 else None)
    return x2d.reshape(bsz, seq_len, d_model)
```

```python
import functools
import math

import jax
import jax.numpy as jnp
import numpy as np
from jax import lax
from jax.experimental import pallas as pl
from jax.experimental.pallas import tpu as pltpu

F32 = jnp.float32
BF16 = jnp.bfloat16

NORM_EPS = 1e-6
S5_GROUPS = 32
S5_GROUP_CH = 16
S5_STATE = 64
S5_WIDTH = S5_GROUPS * S5_GROUP_CH
S5_CHUNK = 8
S5_LANE_GROUPS = 8
GLA_HEADS = 4
GLA_HEAD_DIM = 128
GLA_WIDTH = GLA_HEADS * GLA_HEAD_DIM
GLA_LOWRANK = 16
GLA_TAU = 16.0
GLA_CHUNK = 64
ATTN_Q_HEADS = 8
ATTN_KV_HEADS = 2
ATTN_HEAD_DIM = 64
ATTN_GROUP = ATTN_Q_HEADS // ATTN_KV_HEADS
ATTN_WIDTH = ATTN_Q_HEADS * ATTN_HEAD_DIM
ATTN_KV_WIDTH = ATTN_KV_HEADS * ATTN_HEAD_DIM
GRID_W = 64
ROPE_BASE = 10000.0
ROPE_PAIR = ATTN_HEAD_DIM // 4
ATTN_VT_ROWS = ATTN_HEAD_DIM + 16
ATTN_BOUND_MAX = 50.0
ATTN_BOUND_SLACK = 1.0 + 2.0 ** -10

LANE = 128
V7X_VMEM_BYTES = 64 * 1024 * 1024

COL_S5 = 0
COL_GQ = 512
COL_GK = 1024
COL_GV = 1536
COL_GG = 2048
COL_Z = 2560
PROJ_WIDTH = 2688


def _cparams(semantics, vmem_mib):
    return pltpu.CompilerParams(
        dimension_semantics=semantics,
        vmem_limit_bytes=min(vmem_mib * 1024 * 1024, V7X_VMEM_BYTES - 4 * 1024 * 1024),
    )


def _resident(shape):
    nd = len(shape)
    return pl.BlockSpec(shape, lambda *_: (0,) * nd, pipeline_mode=pl.Buffered(1))


def _layer(shape, layer):
    nd = len(shape)
    return pl.BlockSpec((None,) + tuple(shape), lambda *_: (layer,) + (0,) * nd, pipeline_mode=pl.Buffered(1))


def _rms(x, gain):
    ms = jnp.mean(x * x, axis=-1, keepdims=True)
    return x * lax.rsqrt(ms + NORM_EPS) * gain


def _dot(a, b):
    return jnp.dot(a, b, preferred_element_type=F32)


def _dot_nt(a, b):
    return lax.dot_general(a, b, (((1,), (1,)), ((), ())), preferred_element_type=F32)


def _dot_tn(a, b):
    return lax.dot_general(a, b, (((0,), (0,)), ((), ())), preferred_element_type=F32)


def _ffn_body(x_ref, gain_ref, wg_ref, wu_ref, wd_ref, *rest, chunk, final):
    if final:
        fg_ref, o_ref, a_ref = rest
    else:
        o_ref, a_ref = rest
    x = x_ref[...]
    h = _rms(x, gain_ref[...]).astype(BF16)
    d_ff = wg_ref.shape[1]
    for c0 in range(0, d_ff, chunk):
        g = _dot(h, wg_ref[:, c0:c0 + chunk].astype(BF16))
        u = _dot(h, wu_ref[:, c0:c0 + chunk].astype(BF16))
        a_ref[:, c0:c0 + chunk] = (g * jax.nn.sigmoid(g) * u).astype(BF16)
    out = x + 0.5 * _dot(a_ref[...], wd_ref[...].astype(BF16))
    if final:
        out = _rms(out, fg_ref[...])
    o_ref[...] = out


def _ffn(x2d, gain, wg, wu, wd, layer, final_gain=None, *, tm=512, chunk=256):
    t, d = x2d.shape
    f = wg.shape[2]
    final = final_gain is not None
    row = pl.BlockSpec((tm, d), lambda i: (i, 0))
    in_specs = [row, _layer((1, d), layer), _layer((d, f), layer), _layer((d, f), layer), _layer((f, d), layer)]
    args = [x2d, gain, wg, wu, wd]
    if final:
        in_specs.append(_resident((1, d)))
        args.append(final_gain.reshape(1, d))
    return pl.pallas_call(
        functools.partial(_ffn_body, chunk=chunk, final=final),
        grid=(t // tm,),
        in_specs=in_specs,
        out_specs=row,
        out_shape=jax.ShapeDtypeStruct((t, d), F32),
        scratch_shapes=[pltpu.VMEM((tm, f), BF16)],
        compiler_params=_cparams(("parallel",), 48),
        name="ffn",
    )(*args)


def _head_rms(x, ones_ref, gain):
    x2 = x * x
    hi = x2.astype(BF16)
    lo = (x2 - hi.astype(F32)).astype(BF16)
    ss = _dot(hi, ones_ref[...]) + _dot(lo, ones_ref[...])
    return x * lax.rsqrt(ss * (1.0 / ATTN_HEAD_DIM) + NORM_EPS) * gain


def _rope(x, cos, sin_signed):
    n = x.shape[-1]
    lane = lax.broadcasted_iota(jnp.int32, x.shape, 1)
    first = (lane & ROPE_PAIR) == 0
    partner = jnp.where(first, pltpu.roll(x, n - ROPE_PAIR, 1), pltpu.roll(x, ROPE_PAIR, 1))
    return x * cos + partner * sin_signed


def _attn_prep(q, kv, cq_ref, sq_ref, ck_ref, sk_ref, gq_ref, gk_ref, oq_ref, ok_ref, qt_ref, k_ref, vt_ref):
    qt_ref[0] = _rope(_head_rms(q, oq_ref, gq_ref[...]), cq_ref[...], sq_ref[...]).T.astype(BF16)
    k = _rope(_head_rms(kv[:, :ATTN_KV_WIDTH], ok_ref, gk_ref[...]), ck_ref[...], sk_ref[...])
    for g in range(ATTN_KV_HEADS):
        k_ref[0, g] = k[:, g * ATTN_HEAD_DIM:(g + 1) * ATTN_HEAD_DIM].astype(BF16)
    vt = kv[:, ATTN_KV_WIDTH:].T.astype(BF16)
    pad = ATTN_VT_ROWS - ATTN_HEAD_DIM
    ones_row = (lax.broadcasted_iota(jnp.int32, (pad, vt.shape[1]), 0) == 0).astype(F32).astype(BF16)
    for g in range(ATTN_KV_HEADS):
        vt_ref[0, g, :ATTN_HEAD_DIM, :] = vt[g * ATTN_HEAD_DIM:(g + 1) * ATTN_HEAD_DIM]
        vt_ref[0, g, ATTN_HEAD_DIM:, :] = ones_row


def _inproj_body(x_ref, gain_ref, w_ref, cq_ref, sq_ref, ck_ref, sk_ref, gq_ref, gk_ref, oq_ref, ok_ref,
                 o_ref, qt_ref, k_ref, vt_ref, *, chunk):
    h = _rms(x_ref[...], gain_ref[...]).astype(BF16)
    n = o_ref.shape[1]
    q = _dot(h, w_ref[:, n:n + ATTN_WIDTH].astype(BF16))
    kv = _dot(h, w_ref[:, n + ATTN_WIDTH:].astype(BF16))
    _attn_prep(q, kv, cq_ref, sq_ref, ck_ref, sk_ref, gq_ref, gk_ref, oq_ref, ok_ref, qt_ref, k_ref, vt_ref)
    for c0 in range(0, n, chunk):
        c1 = min(c0 + chunk, n)
        o_ref[:, c0:c1] = _dot(h, w_ref[:, c0:c1].astype(BF16))


def _inproj(x2d, seq_len, gain, w, tabs, gq, gk, layer, *, tm=512, chunk=512):
    t, d = x2d.shape
    b, nb = t // seq_len, seq_len // tm
    cq, sq, ck, sk = tabs
    ones_q = jnp.asarray(np.kron(np.eye(ATTN_Q_HEADS), np.ones((ATTN_HEAD_DIM, ATTN_HEAD_DIM))), BF16)
    ones_k = jnp.asarray(np.kron(np.eye(ATTN_KV_HEADS), np.ones((ATTN_HEAD_DIM, ATTN_HEAD_DIM))), BF16)
    tab_q = pl.BlockSpec((tm, ATTN_WIDTH), lambda i: (i % nb, 0))
    tab_k = pl.BlockSpec((tm, ATTN_KV_WIDTH), lambda i: (i % nb, 0))
    return pl.pallas_call(
        functools.partial(_inproj_body, chunk=chunk),
        grid=(t // tm,),
        in_specs=[
            pl.BlockSpec((tm, d), lambda i: (i, 0)), _layer((1, d), layer), _layer(w.shape[1:], layer),
            tab_q, tab_q, tab_k, tab_k,
            _layer((1, ATTN_WIDTH), layer), _layer((1, ATTN_KV_WIDTH), layer),
            _resident((ATTN_WIDTH, ATTN_WIDTH)), _resident((ATTN_KV_WIDTH, ATTN_KV_WIDTH)),
        ],
        out_specs=[
            pl.BlockSpec((tm, PROJ_WIDTH), lambda i: (i, 0)),
            pl.BlockSpec((1, ATTN_WIDTH, tm), lambda i: (i // nb, 0, i % nb)),
            pl.BlockSpec((1, ATTN_KV_HEADS, tm, ATTN_HEAD_DIM), lambda i: (i // nb, 0, i % nb, 0)),
            pl.BlockSpec((1, ATTN_KV_HEADS, ATTN_VT_ROWS, tm), lambda i: (i // nb, 0, 0, i % nb)),
        ],
        out_shape=[
            jax.ShapeDtypeStruct((t, PROJ_WIDTH), F32),
            jax.ShapeDtypeStruct((b, ATTN_WIDTH, seq_len), BF16),
            jax.ShapeDtypeStruct((b, ATTN_KV_HEADS, seq_len, ATTN_HEAD_DIM), BF16),
            jax.ShapeDtypeStruct((b, ATTN_KV_HEADS, ATTN_VT_ROWS, seq_len), BF16),
        ],
        compiler_params=_cparams(("parallel",), 48),
        name="inproj",
    )(x2d, gain, w, cq, sq, ck, sk, gq, gk, ones_q, ones_k)


def _rope_tables(l):
    t = jnp.arange(l, dtype=jnp.int32)
    rows = (t // GRID_W).astype(F32)
    cols = (t % GRID_W).astype(F32)
    half = ATTN_HEAD_DIM // 2
    inv_freq = ROPE_BASE ** (-jnp.arange(half // 2, dtype=F32) * 2.0 / half)
    d = np.arange(ATTN_HEAD_DIM)
    use_rows = jnp.asarray((d // half) == 0)
    freq = inv_freq[jnp.asarray(d % (half // 2))]
    second = jnp.asarray((d % half) >= half // 2)
    ang = jnp.where(use_rows[None, :], rows[:, None], cols[:, None]) * freq[None, :]
    cos = jnp.cos(ang)
    sin = jnp.sin(ang)
    sin = jnp.where(second[None, :], sin, -sin)
    scale = ATTN_HEAD_DIM ** -0.5 * math.log2(math.e)
    return (jnp.tile(cos, (1, ATTN_Q_HEADS)) * scale, jnp.tile(sin, (1, ATTN_Q_HEADS)) * scale,
            jnp.tile(cos, (1, ATTN_KV_HEADS)), jnp.tile(sin, (1, ATTN_KV_HEADS)))


def _attn_body(qt_ref, k_ref, vt_ref, o_ref, kmax_scr, *, kb):
    n_kb = k_ref.shape[2] // kb
    items = [(h, j) for h in range(ATTN_GROUP) for j in range(n_kb)]
    heads = [slice(h * ATTN_HEAD_DIM, (h + 1) * ATTN_HEAD_DIM) for h in range(ATTN_GROUP)]

    def scores(h, j):
        return _dot(k_ref[0, 0, j * kb:(j + 1) * kb, :], qt_ref[0, heads[h], :])

    def values(j, p):
        return _dot(vt_ref[0, 0, :, j * kb:(j + 1) * kb], p.astype(BF16))

    def finish(accs):
        outs = [(a[:ATTN_HEAD_DIM] / a[ATTN_HEAD_DIM:ATTN_HEAD_DIM + 1]).T for a in accs]
        o_ref[0] = jnp.concatenate(outs, axis=-1).astype(o_ref.dtype)

    @pl.when(pl.program_id(2) == 0)
    def _():
        kf = k_ref[0, 0].astype(F32)
        k_max2 = jnp.max(jnp.sum(kf * kf, axis=-1, keepdims=True), axis=0, keepdims=True)
        kmax_scr[...] = jnp.broadcast_to(k_max2, kmax_scr.shape)

    k_max2 = kmax_scr[0:1, 0:1]
    qf = qt_ref[0].astype(F32)
    bounds = [jnp.sqrt(jnp.sum(qf[hs] * qf[hs], axis=0, keepdims=True) * k_max2) * ATTN_BOUND_SLACK
              for hs in heads]
    worst = bounds[0]
    for b in bounds[1:]:
        worst = jnp.maximum(worst, b)
    bounded = jnp.max(worst) <= ATTN_BOUND_MAX

    @pl.when(bounded)
    def _():
        accs = [None] * ATTN_GROUP
        st = scores(*items[0])
        for idx, (h, j) in enumerate(items):
            st_next = scores(*items[idx + 1]) if idx + 1 < len(items) else None
            blk = values(j, jnp.exp2(st - bounds[h]))
            accs[h] = blk if j == 0 else accs[h] + blk
            st = st_next
        finish(accs)

    @pl.when(jnp.logical_not(bounded))
    def _():
        accs = [None] * ATTN_GROUP
        st = scores(*items[0])
        run_max = None
        for idx, (h, j) in enumerate(items):
            st_next = scores(*items[idx + 1]) if idx + 1 < len(items) else None
            blk_max = jnp.max(st, axis=0, keepdims=True)
            new_max = blk_max if j == 0 else jnp.maximum(run_max, blk_max)
            blk = values(j, jnp.exp2(st - new_max))
            accs[h] = blk if j == 0 else jnp.exp2(run_max - new_max) * accs[h] + blk
            run_max = new_max
            st = st_next
        finish(accs)


def _attention(qt, k, vt, *, tq=1024, kb=512):
    b, _, l = qt.shape
    tq, kb = min(tq, l), min(kb, l)
    assert l % tq == 0 and l % kb == 0
    gw = ATTN_GROUP * ATTN_HEAD_DIM
    return pl.pallas_call(
        functools.partial(_attn_body, kb=kb),
        grid=(b, ATTN_KV_HEADS, l // tq),
        in_specs=[
            pl.BlockSpec((1, gw, tq), lambda bi, g, i: (bi, g, i)),
            pl.BlockSpec((1, 1, l, ATTN_HEAD_DIM), lambda bi, g, i: (bi, g, 0, 0)),
            pl.BlockSpec((1, 1, ATTN_VT_ROWS, l), lambda bi, g, i: (bi, g, 0, 0)),
        ],
        out_specs=pl.BlockSpec((1, tq, gw), lambda bi, g, i: (bi, i, g)),
        out_shape=jax.ShapeDtypeStruct((b, l, ATTN_WIDTH), BF16),
        scratch_shapes=[pltpu.VMEM((8, LANE), F32)],
        compiler_params=_cparams(("parallel", "parallel", "arbitrary"), 48),
        name="attention",
    )(qt, k, vt)


def _gla_body(qf_ref, kf_ref, vf_ref, zf_ref, qb_ref, kb_ref, vb_ref, zb_ref,
              waf_ref, wab_ref, baf_ref, bab_ref, cumf_ref, cumb_ref,
              of_ref, ob_ref, st_ref):
    @pl.when(pl.program_id(1) == 0)
    def _():
        st_ref[...] = jnp.zeros_like(st_ref)

    tb = qf_ref.shape[1]
    n_chunks = tb // GLA_CHUNK
    both = (0, 1)
    heads = [slice(h * GLA_HEAD_DIM, (h + 1) * GLA_HEAD_DIM) for h in range(GLA_HEADS)]
    chunks = [slice(ci * GLA_CHUNK, (ci + 1) * GLA_CHUNK) for ci in range(n_chunks)]
    q_refs, k_refs, v_refs, z_refs = (qf_ref, qb_ref), (kf_ref, kb_ref), (vf_ref, vb_ref), (zf_ref, zb_ref)
    wa_refs, ba_refs, cum_refs, o_refs = (waf_ref, wab_ref), (baf_ref, bab_ref), (cumf_ref, cumb_ref), (of_ref, ob_ref)

    logits = [_dot(z_refs[d][0].astype(BF16), wa_refs[d][...]) + ba_refs[d][...] for d in both]
    log_a = [(jnp.minimum(x, 0.0) - jnp.log(1.0 + jnp.exp(-jnp.abs(x)))) * (1.0 / GLA_TAU) for x in logits]
    hi = [x.astype(BF16) for x in log_a]
    lo = [(x - h.astype(F32)).astype(BF16) for x, h in zip(log_a, hi)]
    bcum = [_dot(cum_refs[d][...], hi[d]) + _dot(cum_refs[d][...], lo[d]) for d in both]
    q_dec, k_dec, k_end, decay, vb, mask = [], [], [], [], [], []
    r = lax.broadcasted_iota(jnp.int32, (tb, tb), 0)
    c = lax.broadcasted_iota(jnp.int32, (tb, tb), 1)
    same = (r // GLA_CHUNK) == (c // GLA_CHUNK)
    for d in both:
        edge = 0 if d else GLA_CHUNK - 1
        btot = jnp.concatenate(
            [jnp.broadcast_to(bcum[d][rows.start + edge:rows.start + edge + 1], (GLA_CHUNK, bcum[d].shape[1]))
             for rows in chunks], axis=0)
        k = k_refs[d][0]
        q_dec.append(((q_refs[d][0] * GLA_HEAD_DIM ** -0.5) * jnp.exp(bcum[d])).astype(BF16))
        k_dec.append((k * jnp.exp(-bcum[d])).astype(BF16))
        k_end.append((k * jnp.exp(btot - bcum[d])).astype(BF16))
        decay.append(jnp.exp(btot))
        vb.append(v_refs[d][0].astype(BF16))
        mask.append(same & ((c >= r) if d else (c <= r)))

    scores = [[jnp.where(mask[d], _dot_nt(q_dec[d][:, hs], k_dec[d][:, hs]), 0.0).astype(BF16) for hs in heads]
              for d in both]
    o_intra = [[_dot(scores[d][h], vb[d][:, hs]) for h, hs in enumerate(heads)] for d in both]
    kv = [[[_dot_tn(vb[d][rows, hs], k_end[d][rows, hs]) for rows in chunks] for hs in heads] for d in both]

    enter = [[[None] * n_chunks for _ in heads] for _ in both]
    for d in both:
        order = range(n_chunks - 1, -1, -1) if d else range(n_chunks)
        for h, hs in enumerate(heads):
            st = st_ref[d, h]
            for ci in order:
                enter[d][h][ci] = st.astype(BF16)
                st = decay[d][chunks[ci].start:chunks[ci].start + 1, hs] * st + kv[d][h][ci]
            st_ref[d, h] = st

    o_inter = [[[_dot_nt(q_dec[d][rows, hs], enter[d][h][ci]) for ci, rows in enumerate(chunks)]
                for h, hs in enumerate(heads)] for d in both]
    for d in both:
        o_refs[d][0] = jnp.concatenate(
            [o_intra[d][h] + jnp.concatenate(o_inter[d][h], axis=0) for h in range(GLA_HEADS)], axis=-1)


def _gla_gate_params(w_alpha, b_alpha):
    depth = w_alpha.shape[0]
    r = GLA_LOWRANK
    zeros = jnp.zeros((depth, LANE, GLA_WIDTH), F32)
    waf = zeros.at[:, :r].set(w_alpha[:, 0]).astype(BF16)
    wab = zeros.at[:, r:2 * r].set(w_alpha[:, 1]).astype(BF16)
    return waf, wab, b_alpha[:, 0].reshape(depth, 1, GLA_WIDTH), b_alpha[:, 1].reshape(depth, 1, GLA_WIDTH)


def _gla(proj, gate_params, layer, *, tb=256):
    b, l, _ = proj.shape
    nblk = l // tb
    idx = np.arange(tb)
    same = (idx[:, None] // GLA_CHUNK) == (idx[None, :] // GLA_CHUNK)
    cum_f = jnp.asarray(same & (idx[None, :] <= idx[:, None]), BF16)
    cum_b = jnp.asarray(same & (idx[None, :] >= idx[:, None]), BF16)

    def col(base, width, rev):
        if rev:
            return pl.BlockSpec((1, tb, width), lambda bi, i: (bi, nblk - 1 - i, base // width))
        return pl.BlockSpec((1, tb, width), lambda bi, i: (bi, i, base // width))

    w = GLA_WIDTH
    in_specs = [col(COL_GQ, w, False), col(COL_GK, w, False), col(COL_GV, w, False), col(COL_Z, LANE, False),
                col(COL_GQ, w, True), col(COL_GK, w, True), col(COL_GV, w, True), col(COL_Z, LANE, True),
                _layer((LANE, w), layer), _layer((LANE, w), layer), _layer((1, w), layer), _layer((1, w), layer),
                _resident((tb, tb)), _resident((tb, tb))]
    return pl.pallas_call(
        _gla_body,
        grid=(b, nblk),
        in_specs=in_specs,
        out_specs=[col(0, w, False), col(0, w, True)],
        out_shape=[jax.ShapeDtypeStruct((b, l, GLA_WIDTH), F32)] * 2,
        scratch_shapes=[pltpu.VMEM((2, GLA_HEADS, GLA_HEAD_DIM, GLA_HEAD_DIM), F32)],
        compiler_params=_cparams(("parallel", "arbitrary"), 32),
        name="gla",
    )(*([proj] * 8), *gate_params, cum_f, cum_b)


def _s5_expand(comb_ref, winc_ref, woc_ref, tz_scr, win_scr, wout_scr):
    tc, lg, h, p = S5_CHUNK, S5_LANE_GROUPS, S5_GROUP_CH, S5_STATE
    nst = lg * p
    row_g = lax.broadcasted_iota(jnp.int32, (LANE, 1), 0) // h
    same = row_g == lax.broadcasted_iota(jnp.int32, (1, LANE), 1) // h
    blocks = [jnp.where(same, jnp.concatenate([comb_ref[lag]] * lg, axis=0), 0.0).astype(BF16)
              for lag in range(2 * tc - 1)]
    for tp in range(tc):
        for t in range(tc):
            tz_scr[tp * LANE:(tp + 1) * LANE, t * LANE:(t + 1) * LANE] = blocks[t - tp + tc - 1]
    same = row_g == (lax.broadcasted_iota(jnp.int32, (1, 4 * nst), 1) % nst) // p
    for tp in range(tc):
        slab = winc_ref[tp * h:(tp + 1) * h, :]
        win_scr[tp * LANE:(tp + 1) * LANE, :] = jnp.where(
            same, jnp.concatenate([slab] * lg, axis=0), 0.0).astype(BF16)
    lane_g = (lax.broadcasted_iota(jnp.int32, (1, tc * LANE), 1) // h) % lg
    for part in range(4):
        tab = woc_ref[part * p:(part + 1) * p, :]
        for g in range(lg):
            r0 = (part * lg + g) * p
            wout_scr[r0:r0 + p, :] = jnp.where(lane_g == g, tab, 0.0).astype(BF16)


def _s5_body(u_ref, comb_ref, winc_ref, woc_ref, a_ref, d_ref, o_ref, tz_scr, win_scr, wout_scr, z_scr, x_scr):
    @pl.when(pl.program_id(1) == 0)
    def _():
        _s5_expand(comb_ref, winc_ref, woc_ref, tz_scr, win_scr, wout_scr)

    l = u_ref.shape[1]
    nc = l // S5_CHUNK
    nst = S5_LANE_GROUPS * S5_STATE
    uk = jnp.concatenate([u_ref[0, pl.ds(t, nc, stride=S5_CHUNK), :] for t in range(S5_CHUNK)], axis=-1)
    ukb = uk.astype(BF16)
    z_scr[...] = _dot(ukb, win_scr[...])
    y_local = _dot(ukb, tz_scr[...]) + uk * d_ref[...]

    a = a_ref[...]
    afr, afi, abr, abi = (a[:, i * nst:(i + 1) * nst] for i in range(4))
    ntile = nc // 8

    def tile_step(i, carry):
        sfr, sfi, sbr, sbi = carry
        rf = pl.multiple_of(i * 8, 8)
        rb = pl.multiple_of((ntile - 1 - i) * 8, 8)
        zf = z_scr[pl.ds(rf, 8), 0:2 * nst]
        zb = z_scr[pl.ds(rb, 8), 2 * nst:4 * nst]
        xfr, xfi, xbr, xbi = [], [], [None] * 8, [None] * 8
        for r in range(8):
            xfr.append(sfr)
            xfi.append(sfi)
            sfr, sfi = (afr * sfr - afi * sfi + zf[r:r + 1, :nst],
                        afr * sfi + afi * sfr + zf[r:r + 1, nst:])
            q = 7 - r
            xbr[q] = sbr
            xbi[q] = sbi
            sbr, sbi = (abr * sbr - abi * sbi + zb[q:q + 1, :nst],
                        abr * sbi + abi * sbr + zb[q:q + 1, nst:])
        x_scr[pl.ds(rf, 8), 0:nst] = jnp.concatenate(xfr, axis=0)
        x_scr[pl.ds(rf, 8), nst:2 * nst] = jnp.concatenate(xfi, axis=0)
        x_scr[pl.ds(rb, 8), 2 * nst:3 * nst] = jnp.concatenate(xbr, axis=0)
        x_scr[pl.ds(rb, 8), 3 * nst:4 * nst] = jnp.concatenate(xbi, axis=0)
        return sfr, sfi, sbr, sbi

    zero = jnp.zeros((1, nst), F32)
    lax.fori_loop(0, ntile, tile_step, (zero, zero, zero, zero), unroll=True)

    y = y_local + _dot(x_scr[...].astype(BF16), wout_scr[...])
    for t in range(S5_CHUNK):
        o_ref[0, pl.ds(t, nc, stride=S5_CHUNK), :] = y[:, t * LANE:(t + 1) * LANE]


def _s5(proj, tables, d_tiled, layer):
    b, l, _ = proj.shape
    comb, winc, woc, a = tables
    nk = S5_WIDTH // LANE
    nc = l // S5_CHUNK
    feat = S5_CHUNK * LANE
    nst4 = 4 * S5_LANE_GROUPS * S5_STATE
    per_block = lambda arr: pl.BlockSpec((None, None) + arr.shape[2:], lambda k, bi: (layer, k, 0, 0))
    return pl.pallas_call(
        _s5_body,
        grid=(nk, b),
        in_specs=[
            pl.BlockSpec((1, l, LANE), lambda k, bi: (bi, 0, COL_S5 // LANE + k)),
            pl.BlockSpec((None,) + comb.shape[1:3] + (LANE,), lambda k, bi: (layer, 0, 0, k)),
            per_block(winc), per_block(woc), per_block(a), per_block(d_tiled),
        ],
        out_specs=pl.BlockSpec((1, l, LANE), lambda k, bi: (bi, 0, k)),
        out_shape=jax.ShapeDtypeStruct((b, l, S5_WIDTH), F32),
        scratch_shapes=[pltpu.VMEM((feat, feat), BF16), pltpu.VMEM((feat, nst4), BF16), pltpu.VMEM((nst4, feat), BF16),
                        pltpu.VMEM((nc, nst4), F32), pltpu.VMEM((nc, nst4), F32)],
        compiler_params=_cparams(("arbitrary", "arbitrary"), 48),
        name="s5",
    )(proj, comb, winc, woc, a, d_tiled)


def _cmul(ar, ai, br, bi):
    return ar * br - ai * bi, ar * bi + ai * br


def _s5_tables(lam_re, lam_im, log_dt, b_re, b_im, c_re, c_im):
    g, p, h, tc, lg = S5_GROUPS, S5_STATE, S5_GROUP_CH, S5_CHUNK, S5_LANE_GROUPS
    nk = g // lg
    dt = jnp.exp(log_dt)[..., None]
    mag = jnp.exp(lam_re * dt)
    lbr, lbi = mag * jnp.cos(lam_im * dt), mag * jnp.sin(lam_im * dt)
    den = lam_re * lam_re + lam_im * lam_im
    nr, ni = lbr - 1.0, lbi
    fr, fi = (nr * lam_re + ni * lam_im) / den, (ni * lam_re - nr * lam_im) / den
    bbr, bbi = _cmul(fr[..., None], fi[..., None], b_re, b_im)
    pwr, pwi = [jnp.ones_like(lbr)], [jnp.zeros_like(lbi)]
    for _ in range(tc):
        nr_, ni_ = _cmul(pwr[-1], pwi[-1], lbr, lbi)
        pwr.append(nr_)
        pwi.append(ni_)
    pwr, pwi = jnp.stack(pwr, 1), jnp.stack(pwi, 1)

    kern = []
    for d in range(2):
        lanes = lambda x: jnp.repeat(x, h, axis=-1)
        cr = c_re[d].transpose(2, 0, 1).reshape(p, 1, 1, g * h)
        ci = c_im[d].transpose(2, 0, 1).reshape(p, 1, 1, g * h)
        pr = lanes(pwr[d, :tc].transpose(2, 0, 1)).reshape(p, tc, 1, g * h)
        pi = lanes(pwi[d, :tc].transpose(2, 0, 1)).reshape(p, tc, 1, g * h)
        br = lanes(bbr[d].transpose(1, 2, 0)).reshape(p, 1, h, g * h)
        bi = lanes(bbi[d].transpose(1, 2, 0)).reshape(p, 1, h, g * h)
        mr, mi = _cmul(cr, ci, pr, pi)
        kern.append(jnp.sum(mr * br - mi * bi, axis=0))
    comb = jnp.concatenate([kern[1][tc - 1:0:-1], (kern[0][0] + kern[1][0])[None], kern[0][1:tc]])

    parts = []
    for d, sel in ((0, np.arange(tc - 1, -1, -1)), (1, np.arange(tc))):
        r_, i_ = _cmul(pwr[d][sel][..., None], pwi[d][sel][..., None], bbr[d][None], bbi[d][None])
        parts += [r_, i_]
    w = jnp.stack([x.transpose(0, 3, 1, 2).reshape(tc, h, nk, lg * p) for x in parts])
    winc = w.transpose(3, 1, 2, 0, 4).reshape(nk, tc * h, 4 * lg * p)

    parts = []
    for d, sel in ((0, np.arange(1, tc + 1)), (1, np.arange(tc, 0, -1))):
        r_, i_ = _cmul(c_re[d][None], c_im[d][None], pwr[d][sel][:, :, None, :], pwi[d][sel][:, :, None, :])
        parts += [r_, -i_]
    wo = jnp.stack([x.reshape(tc, nk, lg, h, p).transpose(1, 4, 0, 2, 3).reshape(nk, p, tc * lg * h) for x in parts], 1)
    woc = wo.reshape(nk, 4 * p, tc * lg * h)

    a = jnp.stack([pwr[0, tc], pwi[0, tc], pwr[1, tc], pwi[1, tc]])
    a = a.reshape(4, nk, lg * p).transpose(1, 0, 2).reshape(nk, 1, 4 * lg * p)
    return comb, winc, woc, a


def _merge_body(x_ref, ys_ref, of_ref, ob_ref, gate_ref, ya_ref, mg_ref, wm_ref, bm_ref, wglu_ref,
                gg_ref, wbs_ref, wbg_ref, wba_ref, wo_ref, o_ref):
    x = x_ref[...]
    d = x.shape[1]
    h = _rms(x, mg_ref[...]).astype(BF16)

    y = jax.nn.gelu(ys_ref[...])
    y_s5 = y * jax.nn.sigmoid(_dot(y.astype(BF16), wglu_ref[...].astype(BF16)))

    o = of_ref[...] + ob_ref[...]
    heads = []
    for hh in range(GLA_HEADS):
        oh = o[:, hh * GLA_HEAD_DIM:(hh + 1) * GLA_HEAD_DIM]
        heads.append(oh * lax.rsqrt(jnp.mean(oh * oh, axis=-1, keepdims=True) + NORM_EPS))
    gate = gate_ref[...]
    y_gla = jnp.concatenate(heads, axis=-1) * gg_ref[...] * (gate * jax.nn.sigmoid(gate))

    merged = None
    for i, (val, w_ref) in enumerate(((y_s5.astype(BF16), wbs_ref), (y_gla.astype(BF16), wbg_ref),
                                      (ya_ref[...], wba_ref))):
        g = jax.nn.sigmoid(_dot(h, wm_ref[:, i * d:(i + 1) * d].astype(BF16)) + bm_ref[:, i * d:(i + 1) * d])
        term = g * _dot(val, w_ref[...].astype(BF16))
        merged = term if merged is None else merged + term
    o_ref[...] = x + _dot(merged.astype(BF16), wo_ref[...].astype(BF16))


def _merge(x2d, ys5, o_f, o_b, proj2d, y_attn, mix_gain, w_merge, b_merge, w_glu, gla_gain,
           wb_s5, wb_gla, wb_attn, w_out, layer, *, tm=512):
    t, d = x2d.shape
    per_layer = lambda a: _layer(a.shape[1:], layer)
    row = lambda w: pl.BlockSpec((tm, w), lambda i: (i, 0))
    return pl.pallas_call(
        _merge_body,
        grid=(t // tm,),
        in_specs=[
            row(d), row(S5_WIDTH), row(GLA_WIDTH), row(GLA_WIDTH),
            pl.BlockSpec((tm, GLA_WIDTH), lambda i: (i, COL_GG // GLA_WIDTH)),
            row(ATTN_WIDTH),
            per_layer(mix_gain), per_layer(w_merge), per_layer(b_merge), per_layer(w_glu), per_layer(gla_gain),
            per_layer(wb_s5), per_layer(wb_gla), per_layer(wb_attn), per_layer(w_out),
        ],
        out_specs=row(d),
        out_shape=jax.ShapeDtypeStruct((t, d), F32),
        compiler_params=_cparams(("parallel",), 48),
        name="merge",
    )(x2d, ys5, o_f, o_b, proj2d, y_attn, mix_gain, w_merge, b_merge, w_glu, gla_gain,
      wb_s5, wb_gla, wb_attn, w_out)


def _reorder_w_in(w_in):
    z1 = COL_Z + 2 * GLA_LOWRANK
    pad = jnp.zeros(w_in.shape[:-1] + (PROJ_WIDTH - z1,), w_in.dtype)
    return jnp.concatenate([w_in[..., :z1], pad, w_in[..., z1:]], axis=-1)


def kernel(x, ffn1_norm, ffn1_w_gate, ffn1_w_up, ffn1_w_down, mix_norm, w_in, s5_lambda_re, s5_lambda_im, s5_log_dt, s5_b_re, s5_b_im, s5_c_re, s5_c_im, s5_d, s5_w_glu, gla_w_alpha, gla_b_alpha, gla_norm, attn_q_norm, attn_k_norm, w_branch_s5, w_branch_gla, w_branch_attn, w_merge_gate, b_merge_gate, w_out, ffn2_norm, ffn2_w_gate, ffn2_w_up, ffn2_w_down, final_norm):
    bsz, seq_len, d_model = x.shape
    depth = w_in.shape[0]
    tabs = _rope_tables(seq_len)
    vec = lambda g: g.reshape(depth, 1, -1)
    ffn1 = (vec(ffn1_norm), ffn1_w_gate, ffn1_w_up, ffn1_w_down)
    ffn2 = (vec(ffn2_norm), ffn2_w_gate, ffn2_w_up, ffn2_w_down)
    mix_gain = vec(mix_norm)
    w_proj = _reorder_w_in(w_in)
    s5_tabs = jax.vmap(_s5_tables)(s5_lambda_re, s5_lambda_im, s5_log_dt, s5_b_re, s5_b_im, s5_c_re, s5_c_im)
    s5_d_tiled = jnp.tile(s5_d.reshape(depth, S5_WIDTH // LANE, 1, LANE), (1, 1, 1, S5_CHUNK))
    gla_gates = _gla_gate_params(gla_w_alpha, gla_b_alpha)
    attn_gq = vec(jnp.tile(attn_q_norm, (1, ATTN_Q_HEADS)))
    attn_gk = vec(jnp.tile(attn_k_norm, (1, ATTN_KV_HEADS)))
    merge_params = (mix_gain, w_merge_gate, vec(b_merge_gate), s5_w_glu, vec(jnp.tile(gla_norm, (1, GLA_HEADS))),
                    w_branch_s5, w_branch_gla, w_branch_attn, w_out)

    flat = lambda a: a.reshape(bsz * seq_len, a.shape[-1])
    x2d = flat(x)
    for i in range(depth):
        x2d = _ffn(x2d, *ffn1, i)

        proj2d, qt, k, vt = _inproj(x2d, seq_len, mix_gain, w_proj, tabs, attn_gq, attn_gk, i)
        proj = proj2d.reshape(bsz, seq_len, PROJ_WIDTH)
        ys5 = _s5(proj, s5_tabs, s5_d_tiled, i)
        o_f, o_b = _gla(proj, gla_gates, i)
        y_attn = _attention(qt, k, vt)
        x2d = _merge(x2d, flat(ys5), flat(o_f), flat(o_b), proj2d, flat(y_attn), *merge_params, i)

        x2d = _ffn(x2d, *ffn2, i, final_norm if i == depth - 1 else None)
    return x2d.reshape(bsz, seq_len, d_model)
```

```python
import functools
import math

import jax
import jax.numpy as jnp
import numpy as np
from jax import lax
from jax.experimental import pallas as pl
from jax.experimental.pallas import tpu as pltpu

F32 = jnp.float32
BF16 = jnp.bfloat16

NORM_EPS = 1e-6
S5_GROUPS = 32
S5_GROUP_CH = 16
S5_STATE = 64
S5_WIDTH = S5_GROUPS * S5_GROUP_CH
S5_CHUNK = 8
S5_LANE_GROUPS = 8
GLA_HEADS = 4
GLA_HEAD_DIM = 128
GLA_WIDTH = GLA_HEADS * GLA_HEAD_DIM
GLA_LOWRANK = 16
GLA_TAU = 16.0
GLA_CHUNK = 64
ATTN_Q_HEADS = 8
ATTN_KV_HEADS = 2
ATTN_HEAD_DIM = 64
ATTN_GROUP = ATTN_Q_HEADS // ATTN_KV_HEADS
ATTN_WIDTH = ATTN_Q_HEADS * ATTN_HEAD_DIM
ATTN_KV_WIDTH = ATTN_KV_HEADS * ATTN_HEAD_DIM
GRID_W = 64
ROPE_BASE = 10000.0
ROPE_PAIR = ATTN_HEAD_DIM // 4
ATTN_VT_ROWS = ATTN_HEAD_DIM + 16
ATTN_BOUND_MAX = 50.0
ATTN_BOUND_SLACK = 1.0 + 2.0 ** -10

LANE = 128
V7X_VMEM_BYTES = 64 * 1024 * 1024

COL_S5 = 0
COL_GQ = 512
COL_GK = 1024
COL_GV = 1536
COL_GG = 2048
COL_Z = 2560
PROJ_WIDTH = 2688


def _cparams(semantics, vmem_mib):
    return pltpu.CompilerParams(
        dimension_semantics=semantics,
        vmem_limit_bytes=min(vmem_mib * 1024 * 1024, V7X_VMEM_BYTES - 4 * 1024 * 1024),
    )


def _resident(shape):
    nd = len(shape)
    return pl.BlockSpec(shape, lambda *_: (0,) * nd, pipeline_mode=pl.Buffered(1))


def _layer(shape, layer):
    nd = len(shape)
    return pl.BlockSpec((None,) + tuple(shape), lambda *_: (layer,) + (0,) * nd, pipeline_mode=pl.Buffered(1))


def _rms(x, gain):
    ms = jnp.mean(x * x, axis=-1, keepdims=True)
    return x * lax.rsqrt(ms + NORM_EPS) * gain


def _dot(a, b):
    return jnp.dot(a, b, preferred_element_type=F32)


def _dot_nt(a, b):
    return lax.dot_general(a, b, (((1,), (1,)), ((), ())), preferred_element_type=F32)


def _dot_tn(a, b):
    return lax.dot_general(a, b, (((0,), (0,)), ((), ())), preferred_element_type=F32)


def _ffn_body(x_ref, gain_ref, wg_ref, wu_ref, wd_ref, *rest, chunk, final):
    if final:
        fg_ref, o_ref, a_ref = rest
    else:
        o_ref, a_ref = rest
    x = x_ref[...]
    h = _rms(x, gain_ref[...]).astype(BF16)
    d_ff = wg_ref.shape[1]
    for c0 in range(0, d_ff, chunk):
        g = _dot(h, wg_ref[:, c0:c0 + chunk].astype(BF16))
        u = _dot(h, wu_ref[:, c0:c0 + chunk].astype(BF16))
        a_ref[:, c0:c0 + chunk] = (g * jax.nn.sigmoid(g) * u).astype(BF16)
    out = x + 0.5 * _dot(a_ref[...], wd_ref[...].astype(BF16))
    if final:
        out = _rms(out, fg_ref[...])
    o_ref[...] = out


def _ffn(x2d, gain, wg, wu, wd, layer, final_gain=None, *, tm=512, chunk=256):
    t, d = x2d.shape
    f = wg.shape[2]
    final = final_gain is not None
    row = pl.BlockSpec((tm, d), lambda i: (i, 0))
    in_specs = [row, _layer((1, d), layer), _layer((d, f), layer), _layer((d, f), layer), _layer((f, d), layer)]
    args = [x2d, gain, wg, wu, wd]
    if final:
        in_specs.append(_resident((1, d)))
        args.append(final_gain.reshape(1, d))
    return pl.pallas_call(
        functools.partial(_ffn_body, chunk=chunk, final=final),
        grid=(t // tm,),
        in_specs=in_specs,
        out_specs=row,
        out_shape=jax.ShapeDtypeStruct((t, d), F32),
        scratch_shapes=[pltpu.VMEM((tm, f), BF16)],
        compiler_params=_cparams(("parallel",), 48),
        name="ffn",
    )(*args)


def _head_rms(x, ones_ref, gain):
    x2 = x * x
    hi = x2.astype(BF16)
    lo = (x2 - hi.astype(F32)).astype(BF16)
    ss = _dot(hi, ones_ref[...]) + _dot(lo, ones_ref[...])
    return x * lax.rsqrt(ss * (1.0 / ATTN_HEAD_DIM) + NORM_EPS) * gain


def _rope(x, cos, sin_signed):
    n = x.shape[-1]
    lane = lax.broadcasted_iota(jnp.int32, x.shape, 1)
    first = (lane & ROPE_PAIR) == 0
    partner = jnp.where(first, pltpu.roll(x, n - ROPE_PAIR, 1), pltpu.roll(x, ROPE_PAIR, 1))
    return x * cos + partner * sin_signed


def _attn_prep(q, kv, cq_ref, sq_ref, ck_ref, sk_ref, gq_ref, gk_ref, oq_ref, ok_ref, qt_ref, k_ref, vt_ref):
    qt_ref[0] = _rope(_head_rms(q, oq_ref, gq_ref[...]), cq_ref[...], sq_ref[...]).T.astype(BF16)
    k = _rope(_head_rms(kv[:, :ATTN_KV_WIDTH], ok_ref, gk_ref[...]), ck_ref[...], sk_ref[...])
    for g in range(ATTN_KV_HEADS):
        k_ref[0, g] = k[:, g * ATTN_HEAD_DIM:(g + 1) * ATTN_HEAD_DIM].astype(BF16)
    vt = kv[:, ATTN_KV_WIDTH:].T.astype(BF16)
    pad = ATTN_VT_ROWS - ATTN_HEAD_DIM
    ones_row = (lax.broadcasted_iota(jnp.int32, (pad, vt.shape[1]), 0) == 0).astype(F32).astype(BF16)
    for g in range(ATTN_KV_HEADS):
        vt_ref[0, g, :ATTN_HEAD_DIM, :] = vt[g * ATTN_HEAD_DIM:(g + 1) * ATTN_HEAD_DIM]
        vt_ref[0, g, ATTN_HEAD_DIM:, :] = ones_row


def _inproj_body(x_ref, gain_ref, w_ref, cq_ref, sq_ref, ck_ref, sk_ref, gq_ref, gk_ref, oq_ref, ok_ref,
                 o_ref, qt_ref, k_ref, vt_ref, *, chunk):
    h = _rms(x_ref[...], gain_ref[...]).astype(BF16)
    n = o_ref.shape[1]
    q = _dot(h, w_ref[:, n:n + ATTN_WIDTH].astype(BF16))
    kv = _dot(h, w_ref[:, n + ATTN_WIDTH:].astype(BF16))
    _attn_prep(q, kv, cq_ref, sq_ref, ck_ref, sk_ref, gq_ref, gk_ref, oq_ref, ok_ref, qt_ref, k_ref, vt_ref)
    for c0 in range(0, n, chunk):
        c1 = min(c0 + chunk, n)
        o_ref[:, c0:c1] = _dot(h, w_ref[:, c0:c1].astype(BF16))


def _inproj(x2d, seq_len, gain, w, tabs, gq, gk, layer, *, tm=512, chunk=512):
    t, d = x2d.shape
    b, nb = t // seq_len, seq_len // tm
    cq, sq, ck, sk = tabs
    ones_q = jnp.asarray(np.kron(np.eye(ATTN_Q_HEADS), np.ones((ATTN_HEAD_DIM, ATTN_HEAD_DIM))), BF16)
    ones_k = jnp.asarray(np.kron(np.eye(ATTN_KV_HEADS), np.ones((ATTN_HEAD_DIM, ATTN_HEAD_DIM))), BF16)
    tab_q = pl.BlockSpec((tm, ATTN_WIDTH), lambda i: (i % nb, 0))
    tab_k = pl.BlockSpec((tm, ATTN_KV_WIDTH), lambda i: (i % nb, 0))
    return pl.pallas_call(
        functools.partial(_inproj_body, chunk=chunk),
        grid=(t // tm,),
        in_specs=[
            pl.BlockSpec((tm, d), lambda i: (i, 0)), _layer((1, d), layer), _layer(w.shape[1:], layer),
            tab_q, tab_q, tab_k, tab_k,
            _layer((1, ATTN_WIDTH), layer), _layer((1, ATTN_KV_WIDTH), layer),
            _resident((ATTN_WIDTH, ATTN_WIDTH)), _resident((ATTN_KV_WIDTH, ATTN_KV_WIDTH)),
        ],
        out_specs=[
            pl.BlockSpec((tm, PROJ_WIDTH), lambda i: (i, 0)),
            pl.BlockSpec((1, ATTN_WIDTH, tm), lambda i: (i // nb, 0, i % nb)),
            pl.BlockSpec((1, ATTN_KV_HEADS, tm, ATTN_HEAD_DIM), lambda i: (i // nb, 0, i % nb, 0)),
            pl.BlockSpec((1, ATTN_KV_HEADS, ATTN_VT_ROWS, tm), lambda i: (i // nb, 0, 0, i % nb)),
        ],
        out_shape=[
            jax.ShapeDtypeStruct((t, PROJ_WIDTH), F32),
            jax.ShapeDtypeStruct((b, ATTN_WIDTH, seq_len), BF16),
            jax.ShapeDtypeStruct((b, ATTN_KV_HEADS, seq_len, ATTN_HEAD_DIM), BF16),
            jax.ShapeDtypeStruct((b, ATTN_KV_HEADS, ATTN_VT_ROWS, seq_len), BF16),
        ],
        compiler_params=_cparams(("parallel",), 48),
        name="inproj",
    )(x2d, gain, w, cq, sq, ck, sk, gq, gk, ones_q, ones_k)


def _rope_tables(l):
    t = jnp.arange(l, dtype=jnp.int32)
    rows = (t // GRID_W).astype(F32)
    cols = (t % GRID_W).astype(F32)
    half = ATTN_HEAD_DIM // 2
    inv_freq = ROPE_BASE ** (-jnp.arange(half // 2, dtype=F32) * 2.0 / half)
    d = np.arange(ATTN_HEAD_DIM)
    use_rows = jnp.asarray((d // half) == 0)
    freq = inv_freq[jnp.asarray(d % (half // 2))]
    second = jnp.asarray((d % half) >= half // 2)
    ang = jnp.where(use_rows[None, :], rows[:, None], cols[:, None]) * freq[None, :]
    cos = jnp.cos(ang)
    sin = jnp.sin(ang)
    sin = jnp.where(second[None, :], sin, -sin)
    scale = ATTN_HEAD_DIM ** -0.5 * math.log2(math.e)
    return (jnp.tile(cos, (1, ATTN_Q_HEADS)) * scale, jnp.tile(sin, (1, ATTN_Q_HEADS)) * scale,
            jnp.tile(cos, (1, ATTN_KV_HEADS)), jnp.tile(sin, (1, ATTN_KV_HEADS)))


def _attn_body(qt_ref, k_ref, vt_ref, o_ref, kmax_scr, *, kb):
    n_kb = k_ref.shape[2] // kb
    items = [(h, j) for h in range(ATTN_GROUP) for j in range(n_kb)]
    heads = [slice(h * ATTN_HEAD_DIM, (h + 1) * ATTN_HEAD_DIM) for h in range(ATTN_GROUP)]

    def scores(h, j):
        return _dot(k_ref[0, 0, j * kb:(j + 1) * kb, :], qt_ref[0, heads[h], :])

    def values(j, p):
        return _dot(vt_ref[0, 0, :, j * kb:(j + 1) * kb], p.astype(BF16))

    def finish(accs):
        outs = [(a[:ATTN_HEAD_DIM] / a[ATTN_HEAD_DIM:ATTN_HEAD_DIM + 1]).T for a in accs]
        o_ref[0] = jnp.concatenate(outs, axis=-1).astype(o_ref.dtype)

    @pl.when(pl.program_id(2) == 0)
    def _():
        kf = k_ref[0, 0].astype(F32)
        k_max2 = jnp.max(jnp.sum(kf * kf, axis=-1, keepdims=True), axis=0, keepdims=True)
        kmax_scr[...] = jnp.broadcast_to(k_max2, kmax_scr.shape)

    k_max2 = kmax_scr[0:1, 0:1]
    qf = qt_ref[0].astype(F32)
    bounds = [jnp.sqrt(jnp.sum(qf[hs] * qf[hs], axis=0, keepdims=True) * k_max2) * ATTN_BOUND_SLACK
              for hs in heads]
    worst = bounds[0]
    for b in bounds[1:]:
        worst = jnp.maximum(worst, b)
    bounded = jnp.max(worst) <= ATTN_BOUND_MAX

    @pl.when(bounded)
    def _():
        accs = [None] * ATTN_GROUP
        st = scores(*items[0])
        for idx, (h, j) in enumerate(items):
            st_next = scores(*items[idx + 1]) if idx + 1 < len(items) else None
            blk = values(j, jnp.exp2(st - bounds[h]))
            accs[h] = blk if j == 0 else accs[h] + blk
            st = st_next
        finish(accs)

    @pl.when(jnp.logical_not(bounded))
    def _():
        accs = [None] * ATTN_GROUP
        st = scores(*items[0])
        run_max = None
        for idx, (h, j) in enumerate(items):
            st_next = scores(*items[idx + 1]) if idx + 1 < len(items) else None
            blk_max = jnp.max(st, axis=0, keepdims=True)
            new_max = blk_max if j == 0 else jnp.maximum(run_max, blk_max)
            blk = values(j, jnp.exp2(st - new_max))
            accs[h] = blk if j == 0 else jnp.exp2(run_max - new_max) * accs[h] + blk
            run_max = new_max
            st = st_next
        finish(accs)


def _attention(qt, k, vt, *, tq=1024, kb=512):
    b, _, l = qt.shape
    tq, kb = min(tq, l), min(kb, l)
    assert l % tq == 0 and l % kb == 0
    gw = ATTN_GROUP * ATTN_HEAD_DIM
    return pl.pallas_call(
        functools.partial(_attn_body, kb=kb),
        grid=(b, ATTN_KV_HEADS, l // tq),
        in_specs=[
            pl.BlockSpec((1, gw, tq), lambda bi, g, i: (bi, g, i)),
            pl.BlockSpec((1, 1, l, ATTN_HEAD_DIM), lambda bi, g, i: (bi, g, 0, 0)),
            pl.BlockSpec((1, 1, ATTN_VT_ROWS, l), lambda bi, g, i: (bi, g, 0, 0)),
        ],
        out_specs=pl.BlockSpec((1, tq, gw), lambda bi, g, i: (bi, i, g)),
        out_shape=jax.ShapeDtypeStruct((b, l, ATTN_WIDTH), BF16),
        scratch_shapes=[pltpu.VMEM((8, LANE), F32)],
        compiler_params=_cparams(("parallel", "parallel", "arbitrary"), 48),
        name="attention",
    )(qt, k, vt)


def _gla_body(qf_ref, kf_ref, vf_ref, zf_ref, qb_ref, kb_ref, vb_ref, zb_ref,
              waf_ref, wab_ref, baf_ref, bab_ref, cumf_ref, cumb_ref,
              of_ref, ob_ref, st_ref):
    @pl.when(pl.program_id(0) == 0)
    def _():
        st_ref[...] = jnp.zeros_like(st_ref)

    n_batch, tb = qf_ref.shape[0], qf_ref.shape[1]
    n_chunks = tb // GLA_CHUNK
    heads = [slice(h * GLA_HEAD_DIM, (h + 1) * GLA_HEAD_DIM) for h in range(GLA_HEADS)]
    chunks = [slice(ci * GLA_CHUNK, (ci + 1) * GLA_CHUNK) for ci in range(n_chunks)]
    q_refs, k_refs, v_refs, z_refs = (qf_ref, qb_ref), (kf_ref, kb_ref), (vf_ref, vb_ref), (zf_ref, zb_ref)
    wa_refs, ba_refs, cum_refs, o_refs = (waf_ref, wab_ref), (baf_ref, bab_ref), (cumf_ref, cumb_ref), (of_ref, ob_ref)
    r = lax.broadcasted_iota(jnp.int32, (tb, tb), 0)
    c = lax.broadcasted_iota(jnp.int32, (tb, tb), 1)
    same = (r // GLA_CHUNK) == (c // GLA_CHUNK)
    mask = [same & (c <= r), same & (c >= r)]

    def gates(b, d):
        x = _dot(z_refs[d][b].astype(BF16), wa_refs[d][...]) + ba_refs[d][...]
        log_a = (jnp.minimum(x, 0.0) - jnp.log(1.0 + jnp.exp(-jnp.abs(x)))) * (1.0 / GLA_TAU)
        hi = log_a.astype(BF16)
        return hi, (log_a - hi.astype(F32)).astype(BF16)

    def decayed_operands(b, d, hi, lo):
        bcum = _dot(cum_refs[d][...], hi) + _dot(cum_refs[d][...], lo)
        edge = 0 if d else GLA_CHUNK - 1
        btot = jnp.concatenate(
            [jnp.broadcast_to(bcum[rows.start + edge:rows.start + edge + 1], (GLA_CHUNK, bcum.shape[1]))
             for rows in chunks], axis=0)
        k = k_refs[d][b]
        q_dec = ((q_refs[d][b] * GLA_HEAD_DIM ** -0.5) * jnp.exp(bcum)).astype(BF16)
        k_dec = (k * jnp.exp(-bcum)).astype(BF16)
        k_end = (k * jnp.exp(btot - bcum)).astype(BF16)
        return q_dec, k_dec, k_end, jnp.exp(btot), v_refs[d][b].astype(BF16)

    def attend(b, d, q_dec, k_dec, k_end, decay, vb):
        scores = [jnp.where(mask[d], _dot_nt(q_dec[:, hs], k_dec[:, hs]), 0.0).astype(BF16) for hs in heads]
        o_intra = [_dot(scores[h], vb[:, hs]) for h, hs in enumerate(heads)]
        kv = [[_dot_tn(vb[rows, hs], k_end[rows, hs]) for rows in chunks] for hs in heads]
        enter = [[None] * n_chunks for _ in heads]
        order = range(n_chunks - 1, -1, -1) if d else range(n_chunks)
        for h, hs in enumerate(heads):
            st = st_ref[b, d, h]
            for ci in order:
                enter[h][ci] = st.astype(BF16)
                st = decay[chunks[ci].start:chunks[ci].start + 1, hs] * st + kv[h][ci]
            st_ref[b, d, h] = st
        o_inter = [[_dot_nt(q_dec[rows, hs], enter[h][ci]) for ci, rows in enumerate(chunks)]
                   for h, hs in enumerate(heads)]
        o_refs[d][b] = jnp.concatenate(
            [o_intra[h] + jnp.concatenate(o_inter[h], axis=0) for h in range(GLA_HEADS)], axis=-1)

    units = [(b, d) for b in range(n_batch) for d in (0, 1)]
    split, ops = {}, {}
    for t in range(len(units) + 2):
        if t < len(units):
            split[t] = gates(*units[t])
        if 0 <= t - 1 < len(units):
            ops[t - 1] = decayed_operands(*units[t - 1], *split.pop(t - 1))
        if 0 <= t - 2 < len(units):
            attend(*units[t - 2], *ops.pop(t - 2))


def _gla_gate_params(w_alpha, b_alpha):
    depth = w_alpha.shape[0]
    r = GLA_LOWRANK
    zeros = jnp.zeros((depth, LANE, GLA_WIDTH), F32)
    waf = zeros.at[:, :r].set(w_alpha[:, 0]).astype(BF16)
    wab = zeros.at[:, r:2 * r].set(w_alpha[:, 1]).astype(BF16)
    return waf, wab, b_alpha[:, 0].reshape(depth, 1, GLA_WIDTH), b_alpha[:, 1].reshape(depth, 1, GLA_WIDTH)


def _gla(proj, gate_params, layer, *, tb=256):
    b, l, _ = proj.shape
    nblk = l // tb
    idx = np.arange(tb)
    same = (idx[:, None] // GLA_CHUNK) == (idx[None, :] // GLA_CHUNK)
    cum_f = jnp.asarray(same & (idx[None, :] <= idx[:, None]), BF16)
    cum_b = jnp.asarray(same & (idx[None, :] >= idx[:, None]), BF16)

    def col(base, width, rev):
        if rev:
            return pl.BlockSpec((b, tb, width), lambda i: (0, nblk - 1 - i, base // width))
        return pl.BlockSpec((b, tb, width), lambda i: (0, i, base // width))

    w = GLA_WIDTH
    in_specs = [col(COL_GQ, w, False), col(COL_GK, w, False), col(COL_GV, w, False), col(COL_Z, LANE, False),
                col(COL_GQ, w, True), col(COL_GK, w, True), col(COL_GV, w, True), col(COL_Z, LANE, True),
                _layer((LANE, w), layer), _layer((LANE, w), layer), _layer((1, w), layer), _layer((1, w), layer),
                _resident((tb, tb)), _resident((tb, tb))]
    return pl.pallas_call(
        _gla_body,
        grid=(nblk,),
        in_specs=in_specs,
        out_specs=[col(0, w, False), col(0, w, True)],
        out_shape=[jax.ShapeDtypeStruct((b, l, GLA_WIDTH), F32)] * 2,
        scratch_shapes=[pltpu.VMEM((b, 2, GLA_HEADS, GLA_HEAD_DIM, GLA_HEAD_DIM), F32)],
        compiler_params=_cparams(("arbitrary",), 48),
        name="gla",
    )(*([proj] * 8), *gate_params, cum_f, cum_b)


def _s5_expand(comb_ref, winc_ref, woc_ref, tz_scr, win_scr, wout_scr):
    tc, lg, h, p = S5_CHUNK, S5_LANE_GROUPS, S5_GROUP_CH, S5_STATE
    nst = lg * p
    row_g = lax.broadcasted_iota(jnp.int32, (LANE, 1), 0) // h
    same = row_g == lax.broadcasted_iota(jnp.int32, (1, LANE), 1) // h
    blocks = [jnp.where(same, jnp.concatenate([comb_ref[lag]] * lg, axis=0), 0.0).astype(BF16)
              for lag in range(2 * tc - 1)]
    for tp in range(tc):
        for t in range(tc):
            tz_scr[tp * LANE:(tp + 1) * LANE, t * LANE:(t + 1) * LANE] = blocks[t - tp + tc - 1]
    same = row_g == (lax.broadcasted_iota(jnp.int32, (1, 4 * nst), 1) % nst) // p
    for tp in range(tc):
        slab = winc_ref[tp * h:(tp + 1) * h, :]
        win_scr[tp * LANE:(tp + 1) * LANE, :] = jnp.where(
            same, jnp.concatenate([slab] * lg, axis=0), 0.0).astype(BF16)
    lane_g = (lax.broadcasted_iota(jnp.int32, (1, tc * LANE), 1) // h) % lg
    for part in range(4):
        tab = woc_ref[part * p:(part + 1) * p, :]
        for g in range(lg):
            r0 = (part * lg + g) * p
            wout_scr[r0:r0 + p, :] = jnp.where(lane_g == g, tab, 0.0).astype(BF16)


def _s5_body(u_ref, comb_ref, winc_ref, woc_ref, a_ref, d_ref, o_ref, tz_scr, win_scr, wout_scr, z_scr, x_scr):
    @pl.when(pl.program_id(1) == 0)
    def _():
        _s5_expand(comb_ref, winc_ref, woc_ref, tz_scr, win_scr, wout_scr)

    l = u_ref.shape[1]
    nc = l // S5_CHUNK
    nst = S5_LANE_GROUPS * S5_STATE
    uk = jnp.concatenate([u_ref[0, pl.ds(t, nc, stride=S5_CHUNK), :] for t in range(S5_CHUNK)], axis=-1)
    ukb = uk.astype(BF16)
    z_scr[...] = _dot(ukb, win_scr[...])
    y_local = _dot(ukb, tz_scr[...]) + uk * d_ref[...]

    a = a_ref[...]
    afr, afi, abr, abi = (a[:, i * nst:(i + 1) * nst] for i in range(4))
    ntile = nc // 8

    def tile_step(i, carry):
        sfr, sfi, sbr, sbi = carry
        rf = pl.multiple_of(i * 8, 8)
        rb = pl.multiple_of((ntile - 1 - i) * 8, 8)
        zf = z_scr[pl.ds(rf, 8), 0:2 * nst]
        zb = z_scr[pl.ds(rb, 8), 2 * nst:4 * nst]
        xfr, xfi, xbr, xbi = [], [], [None] * 8, [None] * 8
        for r in range(8):
            xfr.append(sfr)
            xfi.append(sfi)
            sfr, sfi = (afr * sfr - afi * sfi + zf[r:r + 1, :nst],
                        afr * sfi + afi * sfr + zf[r:r + 1, nst:])
            q = 7 - r
            xbr[q] = sbr
            xbi[q] = sbi
            sbr, sbi = (abr * sbr - abi * sbi + zb[q:q + 1, :nst],
                        abr * sbi + abi * sbr + zb[q:q + 1, nst:])
        x_scr[pl.ds(rf, 8), 0:nst] = jnp.concatenate(xfr, axis=0)
        x_scr[pl.ds(rf, 8), nst:2 * nst] = jnp.concatenate(xfi, axis=0)
        x_scr[pl.ds(rb, 8), 2 * nst:3 * nst] = jnp.concatenate(xbr, axis=0)
        x_scr[pl.ds(rb, 8), 3 * nst:4 * nst] = jnp.concatenate(xbi, axis=0)
        return sfr, sfi, sbr, sbi

    zero = jnp.zeros((1, nst), F32)
    lax.fori_loop(0, ntile, tile_step, (zero, zero, zero, zero), unroll=True)

    y = y_local + _dot(x_scr[...].astype(BF16), wout_scr[...])
    for t in range(S5_CHUNK):
        o_ref[0, pl.ds(t, nc, stride=S5_CHUNK), :] = y[:, t * LANE:(t + 1) * LANE]


def _s5(proj, tables, d_tiled, layer):
    b, l, _ = proj.shape
    comb, winc, woc, a = tables
    nk = S5_WIDTH // LANE
    nc = l // S5_CHUNK
    feat = S5_CHUNK * LANE
    nst4 = 4 * S5_LANE_GROUPS * S5_STATE
    per_block = lambda arr: pl.BlockSpec((None, None) + arr.shape[2:], lambda k, bi: (layer, k, 0, 0))
    return pl.pallas_call(
        _s5_body,
        grid=(nk, b),
        in_specs=[
            pl.BlockSpec((1, l, LANE), lambda k, bi: (bi, 0, COL_S5 // LANE + k)),
            pl.BlockSpec((None,) + comb.shape[1:3] + (LANE,), lambda k, bi: (layer, 0, 0, k)),
            per_block(winc), per_block(woc), per_block(a), per_block(d_tiled),
        ],
        out_specs=pl.BlockSpec((1, l, LANE), lambda k, bi: (bi, 0, k)),
        out_shape=jax.ShapeDtypeStruct((b, l, S5_WIDTH), F32),
        scratch_shapes=[pltpu.VMEM((feat, feat), BF16), pltpu.VMEM((feat, nst4), BF16), pltpu.VMEM((nst4, feat), BF16),
                        pltpu.VMEM((nc, nst4), F32), pltpu.VMEM((nc, nst4), F32)],
        compiler_params=_cparams(("arbitrary", "arbitrary"), 48),
        name="s5",
    )(proj, comb, winc, woc, a, d_tiled)


def _cmul(ar, ai, br, bi):
    return ar * br - ai * bi, ar * bi + ai * br


def _s5_tables(lam_re, lam_im, log_dt, b_re, b_im, c_re, c_im):
    g, p, h, tc, lg = S5_GROUPS, S5_STATE, S5_GROUP_CH, S5_CHUNK, S5_LANE_GROUPS
    nk = g // lg
    dt = jnp.exp(log_dt)[..., None]
    mag = jnp.exp(lam_re * dt)
    lbr, lbi = mag * jnp.cos(lam_im * dt), mag * jnp.sin(lam_im * dt)
    den = lam_re * lam_re + lam_im * lam_im
    nr, ni = lbr - 1.0, lbi
    fr, fi = (nr * lam_re + ni * lam_im) / den, (ni * lam_re - nr * lam_im) / den
    bbr, bbi = _cmul(fr[..., None], fi[..., None], b_re, b_im)
    pwr, pwi = [jnp.ones_like(lbr)], [jnp.zeros_like(lbi)]
    for _ in range(tc):
        nr_, ni_ = _cmul(pwr[-1], pwi[-1], lbr, lbi)
        pwr.append(nr_)
        pwi.append(ni_)
    pwr, pwi = jnp.stack(pwr, 1), jnp.stack(pwi, 1)

    kern = []
    for d in range(2):
        lanes = lambda x: jnp.repeat(x, h, axis=-1)
        cr = c_re[d].transpose(2, 0, 1).reshape(p, 1, 1, g * h)
        ci = c_im[d].transpose(2, 0, 1).reshape(p, 1, 1, g * h)
        pr = lanes(pwr[d, :tc].transpose(2, 0, 1)).reshape(p, tc, 1, g * h)
        pi = lanes(pwi[d, :tc].transpose(2, 0, 1)).reshape(p, tc, 1, g * h)
        br = lanes(bbr[d].transpose(1, 2, 0)).reshape(p, 1, h, g * h)
        bi = lanes(bbi[d].transpose(1, 2, 0)).reshape(p, 1, h, g * h)
        mr, mi = _cmul(cr, ci, pr, pi)
        kern.append(jnp.sum(mr * br - mi * bi, axis=0))
    comb = jnp.concatenate([kern[1][tc - 1:0:-1], (kern[0][0] + kern[1][0])[None], kern[0][1:tc]])

    parts = []
    for d, sel in ((0, np.arange(tc - 1, -1, -1)), (1, np.arange(tc))):
        r_, i_ = _cmul(pwr[d][sel][..., None], pwi[d][sel][..., None], bbr[d][None], bbi[d][None])
        parts += [r_, i_]
    w = jnp.stack([x.transpose(0, 3, 1, 2).reshape(tc, h, nk, lg * p) for x in parts])
    winc = w.transpose(3, 1, 2, 0, 4).reshape(nk, tc * h, 4 * lg * p)

    parts = []
    for d, sel in ((0, np.arange(1, tc + 1)), (1, np.arange(tc, 0, -1))):
        r_, i_ = _cmul(c_re[d][None], c_im[d][None], pwr[d][sel][:, :, None, :], pwi[d][sel][:, :, None, :])
        parts += [r_, -i_]
    wo = jnp.stack([x.reshape(tc, nk, lg, h, p).transpose(1, 4, 0, 2, 3).reshape(nk, p, tc * lg * h) for x in parts], 1)
    woc = wo.reshape(nk, 4 * p, tc * lg * h)

    a = jnp.stack([pwr[0, tc], pwi[0, tc], pwr[1, tc], pwi[1, tc]])
    a = a.reshape(4, nk, lg * p).transpose(1, 0, 2).reshape(nk, 1, 4 * lg * p)
    return comb, winc, woc, a


def _merge_body(x_ref, ys_ref, of_ref, ob_ref, gate_ref, ya_ref, mg_ref, wm_ref, bm_ref, wglu_ref,
                gg_ref, wbs_ref, wbg_ref, wba_ref, wo_ref, o_ref):
    x = x_ref[...]
    d = x.shape[1]
    h = _rms(x, mg_ref[...]).astype(BF16)

    g_lin = [_dot(h, wm_ref[:, i * d:(i + 1) * d].astype(BF16)) + bm_ref[:, i * d:(i + 1) * d] for i in range(3)]
    p_attn = _dot(ya_ref[...], wba_ref[...].astype(BF16))

    y = jax.nn.gelu(ys_ref[...])
    y_s5 = y * jax.nn.sigmoid(_dot(y.astype(BF16), wglu_ref[...].astype(BF16)))
    p_s5 = _dot(y_s5.astype(BF16), wbs_ref[...].astype(BF16))

    o = of_ref[...] + ob_ref[...]
    heads = []
    for hh in range(GLA_HEADS):
        oh = o[:, hh * GLA_HEAD_DIM:(hh + 1) * GLA_HEAD_DIM]
        heads.append(oh * lax.rsqrt(jnp.mean(oh * oh, axis=-1, keepdims=True) + NORM_EPS))
    gate = gate_ref[...]
    y_gla = jnp.concatenate(heads, axis=-1) * gg_ref[...] * (gate * jax.nn.sigmoid(gate))
    p_gla = _dot(y_gla.astype(BF16), wbg_ref[...].astype(BF16))

    merged = jax.nn.sigmoid(g_lin[0]) * p_s5 + jax.nn.sigmoid(g_lin[1]) * p_gla + jax.nn.sigmoid(g_lin[2]) * p_attn
    o_ref[...] = x + _dot(merged.astype(BF16), wo_ref[...].astype(BF16))


def _merge(x2d, ys5, o_f, o_b, proj2d, y_attn, mix_gain, w_merge, b_merge, w_glu, gla_gain,
           wb_s5, wb_gla, wb_attn, w_out, layer, *, tm=512):
    t, d = x2d.shape
    per_layer = lambda a: _layer(a.shape[1:], layer)
    row = lambda w: pl.BlockSpec((tm, w), lambda i: (i, 0))
    return pl.pallas_call(
        _merge_body,
        grid=(t // tm,),
        in_specs=[
            row(d), row(S5_WIDTH), row(GLA_WIDTH), row(GLA_WIDTH),
            pl.BlockSpec((tm, GLA_WIDTH), lambda i: (i, COL_GG // GLA_WIDTH)),
            row(ATTN_WIDTH),
            per_layer(mix_gain), per_layer(w_merge), per_layer(b_merge), per_layer(w_glu), per_layer(gla_gain),
            per_layer(wb_s5), per_layer(wb_gla), per_layer(wb_attn), per_layer(w_out),
        ],
        out_specs=row(d),
        out_shape=jax.ShapeDtypeStruct((t, d), F32),
        compiler_params=_cparams(("parallel",), 48),
        name="merge",
    )(x2d, ys5, o_f, o_b, proj2d, y_attn, mix_gain, w_merge, b_merge, w_glu, gla_gain,
      wb_s5, wb_gla, wb_attn, w_out)


def _reorder_w_in(w_in):
    z1 = COL_Z + 2 * GLA_LOWRANK
    pad = jnp.zeros(w_in.shape[:-1] + (PROJ_WIDTH - z1,), w_in.dtype)
    return jnp.concatenate([w_in[..., :z1], pad, w_in[..., z1:]], axis=-1)


def kernel(x, ffn1_norm, ffn1_w_gate, ffn1_w_up, ffn1_w_down, mix_norm, w_in, s5_lambda_re, s5_lambda_im, s5_log_dt, s5_b_re, s5_b_im, s5_c_re, s5_c_im, s5_d, s5_w_glu, gla_w_alpha, gla_b_alpha, gla_norm, attn_q_norm, attn_k_norm, w_branch_s5, w_branch_gla, w_branch_attn, w_merge_gate, b_merge_gate, w_out, ffn2_norm, ffn2_w_gate, ffn2_w_up, ffn2_w_down, final_norm):
    bsz, seq_len, d_model = x.shape
    depth = w_in.shape[0]
    tabs = _rope_tables(seq_len)
    vec = lambda g: g.reshape(depth, 1, -1)
    ffn1 = (vec(ffn1_norm), ffn1_w_gate, ffn1_w_up, ffn1_w_down)
    ffn2 = (vec(ffn2_norm), ffn2_w_gate, ffn2_w_up, ffn2_w_down)
    mix_gain = vec(mix_norm)
    w_proj = _reorder_w_in(w_in)
    s5_tabs = jax.vmap(_s5_tables)(s5_lambda_re, s5_lambda_im, s5_log_dt, s5_b_re, s5_b_im, s5_c_re, s5_c_im)
    s5_d_tiled = jnp.tile(s5_d.reshape(depth, S5_WIDTH // LANE, 1, LANE), (1, 1, 1, S5_CHUNK))
    gla_gates = _gla_gate_params(gla_w_alpha, gla_b_alpha)
    attn_gq = vec(jnp.tile(attn_q_norm, (1, ATTN_Q_HEADS)))
    attn_gk = vec(jnp.tile(attn_k_norm, (1, ATTN_KV_HEADS)))
    merge_params = (mix_gain, w_merge_gate, vec(b_merge_gate), s5_w_glu, vec(jnp.tile(gla_norm, (1, GLA_HEADS))),
                    w_branch_s5, w_branch_gla, w_branch_attn, w_out)

    flat = lambda a: a.reshape(bsz * seq_len, a.shape[-1])
    x2d = flat(x)
    for i in range(depth):
        x2d = _ffn(x2d, *ffn1, i)

        proj2d, qt, k, vt = _inproj(x2d, seq_len, mix_gain, w_proj, tabs, attn_gq, attn_gk, i)
        proj = proj2d.reshape(bsz, seq_len, PROJ_WIDTH)
        ys5 = _s5(proj, s5_tabs, s5_d_tiled, i)
        o_f, o_b = _gla(proj, gla_gates, i)
        y_attn = _attention(qt, k, vt)
        x2d = _merge(x2d, flat(ys5), flat(o_f), flat(o_b), proj2d, flat(y_attn), *merge_params, i)

        x2d = _ffn(x2d, *ffn2, i, final_norm if i == depth - 1 else None)
    return x2d.reshape(bsz, seq_len, d_model)
```

```python
import functools
import math

import jax
import jax.numpy as jnp
import numpy as np
from jax import lax
from jax.experimental import pallas as pl
from jax.experimental.pallas import tpu as pltpu

F32 = jnp.float32
BF16 = jnp.bfloat16

NORM_EPS = 1e-6
S5_GROUPS = 32
S5_GROUP_CH = 16
S5_STATE = 64
S5_WIDTH = S5_GROUPS * S5_GROUP_CH
S5_CHUNK = 8
S5_LANE_GROUPS = 8
GLA_HEADS = 4
GLA_HEAD_DIM = 128
GLA_WIDTH = GLA_HEADS * GLA_HEAD_DIM
GLA_LOWRANK = 16
GLA_TAU = 16.0
GLA_CHUNK = 64
ATTN_Q_HEADS = 8
ATTN_KV_HEADS = 2
ATTN_HEAD_DIM = 64
ATTN_GROUP = ATTN_Q_HEADS // ATTN_KV_HEADS
ATTN_WIDTH = ATTN_Q_HEADS * ATTN_HEAD_DIM
ATTN_KV_WIDTH = ATTN_KV_HEADS * ATTN_HEAD_DIM
GRID_W = 64
ROPE_BASE = 10000.0
ROPE_PAIR = ATTN_HEAD_DIM // 4
ATTN_VT_ROWS = ATTN_HEAD_DIM + 16
ATTN_BOUND_MAX = 50.0
ATTN_BOUND_SLACK = 1.0 + 2.0 ** -10

LANE = 128
V7X_VMEM_BYTES = 64 * 1024 * 1024

COL_S5 = 0
COL_GQ = 512
COL_GK = 1024
COL_GV = 1536
COL_GG = 2048
COL_Z = 2560
PROJ_WIDTH = 2688


def _cparams(semantics, vmem_mib):
    return pltpu.CompilerParams(
        dimension_semantics=semantics,
        vmem_limit_bytes=min(vmem_mib * 1024 * 1024, V7X_VMEM_BYTES - 4 * 1024 * 1024),
    )


def _resident(shape):
    nd = len(shape)
    return pl.BlockSpec(shape, lambda *_: (0,) * nd, pipeline_mode=pl.Buffered(1))


def _layer(shape, layer):
    nd = len(shape)
    return pl.BlockSpec((None,) + tuple(shape), lambda *_: (layer,) + (0,) * nd, pipeline_mode=pl.Buffered(1))


def _rms(x, gain):
    ms = jnp.mean(x * x, axis=-1, keepdims=True)
    return x * lax.rsqrt(ms + NORM_EPS) * gain


def _dot(a, b):
    return jnp.dot(a, b, preferred_element_type=F32)


def _dot_nt(a, b):
    return lax.dot_general(a, b, (((1,), (1,)), ((), ())), preferred_element_type=F32)


def _dot_tn(a, b):
    return lax.dot_general(a, b, (((0,), (0,)), ((), ())), preferred_element_type=F32)


def _ffn_body(x_ref, gain_ref, wg_ref, wu_ref, wd_ref, *rest, chunk, final):
    if final:
        fg_ref, o_ref, a_ref = rest
    else:
        o_ref, a_ref = rest
    x = x_ref[...]
    h = _rms(x, gain_ref[...]).astype(BF16)
    d_ff = wg_ref.shape[1]
    for c0 in range(0, d_ff, chunk):
        g = _dot(h, wg_ref[:, c0:c0 + chunk].astype(BF16))
        u = _dot(h, wu_ref[:, c0:c0 + chunk].astype(BF16))
        a_ref[:, c0:c0 + chunk] = (g * jax.nn.sigmoid(g) * u).astype(BF16)
    out = x + 0.5 * _dot(a_ref[...], wd_ref[...].astype(BF16))
    if final:
        out = _rms(out, fg_ref[...])
    o_ref[...] = out


def _ffn(x2d, gain, wg, wu, wd, layer, final_gain=None, *, tm=512, chunk=256):
    t, d = x2d.shape
    f = wg.shape[2]
    final = final_gain is not None
    row = pl.BlockSpec((tm, d), lambda i: (i, 0))
    in_specs = [row, _layer((1, d), layer), _layer((d, f), layer), _layer((d, f), layer), _layer((f, d), layer)]
    args = [x2d, gain, wg, wu, wd]
    if final:
        in_specs.append(_resident((1, d)))
        args.append(final_gain.reshape(1, d))
    return pl.pallas_call(
        functools.partial(_ffn_body, chunk=chunk, final=final),
        grid=(t // tm,),
        in_specs=in_specs,
        out_specs=row,
        out_shape=jax.ShapeDtypeStruct((t, d), F32),
        scratch_shapes=[pltpu.VMEM((tm, f), BF16)],
        compiler_params=_cparams(("parallel",), 48),
        name="ffn",
    )(*args)


def _head_rms(x, ones_ref, gain):
    x2 = x * x
    hi = x2.astype(BF16)
    lo = (x2 - hi.astype(F32)).astype(BF16)
    ss = _dot(hi, ones_ref[...]) + _dot(lo, ones_ref[...])
    return x * lax.rsqrt(ss * (1.0 / ATTN_HEAD_DIM) + NORM_EPS) * gain


def _rope(x, cos, sin_signed):
    n = x.shape[-1]
    lane = lax.broadcasted_iota(jnp.int32, x.shape, 1)
    first = (lane & ROPE_PAIR) == 0
    partner = jnp.where(first, pltpu.roll(x, n - ROPE_PAIR, 1), pltpu.roll(x, ROPE_PAIR, 1))
    return x * cos + partner * sin_signed


def _attn_prep(q, kv, cq_ref, sq_ref, ck_ref, sk_ref, gq_ref, gk_ref, oq_ref, ok_ref, qt_ref, k_ref, vt_ref):
    qt_ref[0] = _rope(_head_rms(q, oq_ref, gq_ref[...]), cq_ref[...], sq_ref[...]).T.astype(BF16)
    k = _rope(_head_rms(kv[:, :ATTN_KV_WIDTH], ok_ref, gk_ref[...]), ck_ref[...], sk_ref[...])
    for g in range(ATTN_KV_HEADS):
        k_ref[0, g] = k[:, g * ATTN_HEAD_DIM:(g + 1) * ATTN_HEAD_DIM].astype(BF16)
    vt = kv[:, ATTN_KV_WIDTH:].T.astype(BF16)
    pad = ATTN_VT_ROWS - ATTN_HEAD_DIM
    ones_row = (lax.broadcasted_iota(jnp.int32, (pad, vt.shape[1]), 0) == 0).astype(F32).astype(BF16)
    for g in range(ATTN_KV_HEADS):
        vt_ref[0, g, :ATTN_HEAD_DIM, :] = vt[g * ATTN_HEAD_DIM:(g + 1) * ATTN_HEAD_DIM]
        vt_ref[0, g, ATTN_HEAD_DIM:, :] = ones_row


def _inproj_body(x_ref, gain_ref, w_ref, cq_ref, sq_ref, ck_ref, sk_ref, gq_ref, gk_ref, oq_ref, ok_ref,
                 o_ref, qt_ref, k_ref, vt_ref, *, chunk):
    h = _rms(x_ref[...], gain_ref[...]).astype(BF16)
    tail = _dot(h, w_ref[:, COL_Z:].astype(BF16))
    a0 = 2 * GLA_LOWRANK
    q = tail[:, a0:a0 + ATTN_WIDTH]
    kv = tail[:, a0 + ATTN_WIDTH:]
    _attn_prep(q, kv, cq_ref, sq_ref, ck_ref, sk_ref, gq_ref, gk_ref, oq_ref, ok_ref, qt_ref, k_ref, vt_ref)
    o_ref[:, COL_Z:] = tail[:, :PROJ_WIDTH - COL_Z]
    for c0 in range(0, COL_Z, chunk):
        o_ref[:, c0:c0 + chunk] = _dot(h, w_ref[:, c0:c0 + chunk].astype(BF16))


def _inproj(x2d, seq_len, gain, w, tabs, gq, gk, layer, *, tm=512, chunk=512):
    t, d = x2d.shape
    b, nb = t // seq_len, seq_len // tm
    cq, sq, ck, sk = tabs
    ones_q = jnp.asarray(np.kron(np.eye(ATTN_Q_HEADS), np.ones((ATTN_HEAD_DIM, ATTN_HEAD_DIM))), BF16)
    ones_k = jnp.asarray(np.kron(np.eye(ATTN_KV_HEADS), np.ones((ATTN_HEAD_DIM, ATTN_HEAD_DIM))), BF16)
    tab_q = pl.BlockSpec((tm, ATTN_WIDTH), lambda i: (i % nb, 0))
    tab_k = pl.BlockSpec((tm, ATTN_KV_WIDTH), lambda i: (i % nb, 0))
    return pl.pallas_call(
        functools.partial(_inproj_body, chunk=chunk),
        grid=(t // tm,),
        in_specs=[
            pl.BlockSpec((tm, d), lambda i: (i, 0)), _layer((1, d), layer), _layer(w.shape[1:], layer),
            tab_q, tab_q, tab_k, tab_k,
            _layer((1, ATTN_WIDTH), layer), _layer((1, ATTN_KV_WIDTH), layer),
            _resident((ATTN_WIDTH, ATTN_WIDTH)), _resident((ATTN_KV_WIDTH, ATTN_KV_WIDTH)),
        ],
        out_specs=[
            pl.BlockSpec((tm, PROJ_WIDTH), lambda i: (i, 0)),
            pl.BlockSpec((1, ATTN_WIDTH, tm), lambda i: (i // nb, 0, i % nb)),
            pl.BlockSpec((1, ATTN_KV_HEADS, tm, ATTN_HEAD_DIM), lambda i: (i // nb, 0, i % nb, 0)),
            pl.BlockSpec((1, ATTN_KV_HEADS, ATTN_VT_ROWS, tm), lambda i: (i // nb, 0, 0, i % nb)),
        ],
        out_shape=[
            jax.ShapeDtypeStruct((t, PROJ_WIDTH), F32),
            jax.ShapeDtypeStruct((b, ATTN_WIDTH, seq_len), BF16),
            jax.ShapeDtypeStruct((b, ATTN_KV_HEADS, seq_len, ATTN_HEAD_DIM), BF16),
            jax.ShapeDtypeStruct((b, ATTN_KV_HEADS, ATTN_VT_ROWS, seq_len), BF16),
        ],
        compiler_params=_cparams(("parallel",), 48),
        name="inproj",
    )(x2d, gain, w, cq, sq, ck, sk, gq, gk, ones_q, ones_k)


def _rope_tables(l):
    t = jnp.arange(l, dtype=jnp.int32)
    rows = (t // GRID_W).astype(F32)
    cols = (t % GRID_W).astype(F32)
    half = ATTN_HEAD_DIM // 2
    inv_freq = ROPE_BASE ** (-jnp.arange(half // 2, dtype=F32) * 2.0 / half)
    d = np.arange(ATTN_HEAD_DIM)
    use_rows = jnp.asarray((d // half) == 0)
    freq = inv_freq[jnp.asarray(d % (half // 2))]
    second = jnp.asarray((d % half) >= half // 2)
    ang = jnp.where(use_rows[None, :], rows[:, None], cols[:, None]) * freq[None, :]
    cos = jnp.cos(ang)
    sin = jnp.sin(ang)
    sin = jnp.where(second[None, :], sin, -sin)
    scale = ATTN_HEAD_DIM ** -0.5 * math.log2(math.e)
    return (jnp.tile(cos, (1, ATTN_Q_HEADS)) * scale, jnp.tile(sin, (1, ATTN_Q_HEADS)) * scale,
            jnp.tile(cos, (1, ATTN_KV_HEADS)), jnp.tile(sin, (1, ATTN_KV_HEADS)))


def _attn_body(qt_ref, k_ref, vt_ref, o_ref, kmax_scr, *, kb):
    n_kb = k_ref.shape[2] // kb
    items = [(h, j) for h in range(ATTN_GROUP) for j in range(n_kb)]
    heads = [slice(h * ATTN_HEAD_DIM, (h + 1) * ATTN_HEAD_DIM) for h in range(ATTN_GROUP)]

    def scores(h, j):
        return _dot(k_ref[0, 0, j * kb:(j + 1) * kb, :], qt_ref[0, heads[h], :])

    def values(j, p):
        return _dot(vt_ref[0, 0, :, j * kb:(j + 1) * kb], p.astype(BF16))

    def finish(accs):
        outs = [(a[:ATTN_HEAD_DIM] / a[ATTN_HEAD_DIM:ATTN_HEAD_DIM + 1]).T for a in accs]
        o_ref[0] = jnp.concatenate(outs, axis=-1).astype(o_ref.dtype)

    @pl.when(pl.program_id(2) == 0)
    def _():
        kf = k_ref[0, 0].astype(F32)
        k_max2 = jnp.max(jnp.sum(kf * kf, axis=-1, keepdims=True), axis=0, keepdims=True)
        kmax_scr[...] = jnp.broadcast_to(k_max2, kmax_scr.shape)

    k_max2 = kmax_scr[0:1, 0:1]
    qf = qt_ref[0].astype(F32)
    bounds = [jnp.sqrt(jnp.sum(qf[hs] * qf[hs], axis=0, keepdims=True) * k_max2) * ATTN_BOUND_SLACK
              for hs in heads]
    worst = bounds[0]
    for b in bounds[1:]:
        worst = jnp.maximum(worst, b)
    bounded = jnp.max(worst) <= ATTN_BOUND_MAX

    @pl.when(bounded)
    def _():
        accs = [None] * ATTN_GROUP
        st = scores(*items[0])
        for idx, (h, j) in enumerate(items):
            st_next = scores(*items[idx + 1]) if idx + 1 < len(items) else None
            blk = values(j, jnp.exp2(st - bounds[h]))
            accs[h] = blk if j == 0 else accs[h] + blk
            st = st_next
        finish(accs)

    @pl.when(jnp.logical_not(bounded))
    def _():
        accs = [None] * ATTN_GROUP
        st = scores(*items[0])
        run_max = None
        for idx, (h, j) in enumerate(items):
            st_next = scores(*items[idx + 1]) if idx + 1 < len(items) else None
            blk_max = jnp.max(st, axis=0, keepdims=True)
            new_max = blk_max if j == 0 else jnp.maximum(run_max, blk_max)
            blk = values(j, jnp.exp2(st - new_max))
            accs[h] = blk if j == 0 else jnp.exp2(run_max - new_max) * accs[h] + blk
            run_max = new_max
            st = st_next
        finish(accs)


def _attention(qt, k, vt, *, tq=1024, kb=512):
    b, _, l = qt.shape
    tq, kb = min(tq, l), min(kb, l)
    assert l % tq == 0 and l % kb == 0
    gw = ATTN_GROUP * ATTN_HEAD_DIM
    return pl.pallas_call(
        functools.partial(_attn_body, kb=kb),
        grid=(b, ATTN_KV_HEADS, l // tq),
        in_specs=[
            pl.BlockSpec((1, gw, tq), lambda bi, g, i: (bi, g, i)),
            pl.BlockSpec((1, 1, l, ATTN_HEAD_DIM), lambda bi, g, i: (bi, g, 0, 0)),
            pl.BlockSpec((1, 1, ATTN_VT_ROWS, l), lambda bi, g, i: (bi, g, 0, 0)),
        ],
        out_specs=pl.BlockSpec((1, tq, gw), lambda bi, g, i: (bi, i, g)),
        out_shape=jax.ShapeDtypeStruct((b, l, ATTN_WIDTH), BF16),
        scratch_shapes=[pltpu.VMEM((8, LANE), F32)],
        compiler_params=_cparams(("parallel", "parallel", "arbitrary"), 48),
        name="attention",
    )(qt, k, vt)


def _gla_body(qf_ref, kf_ref, vf_ref, zf_ref, qb_ref, kb_ref, vb_ref, zb_ref,
              waf_ref, wab_ref, baf_ref, bab_ref, cumf_ref, cumb_ref,
              of_ref, ob_ref, st_ref):
    @pl.when(pl.program_id(0) == 0)
    def _():
        st_ref[...] = jnp.zeros_like(st_ref)

    n_batch, tb = qf_ref.shape[0], qf_ref.shape[1]
    n_chunks = tb // GLA_CHUNK
    heads = [slice(h * GLA_HEAD_DIM, (h + 1) * GLA_HEAD_DIM) for h in range(GLA_HEADS)]
    chunks = [slice(ci * GLA_CHUNK, (ci + 1) * GLA_CHUNK) for ci in range(n_chunks)]
    q_refs, k_refs, v_refs, z_refs = (qf_ref, qb_ref), (kf_ref, kb_ref), (vf_ref, vb_ref), (zf_ref, zb_ref)
    wa_refs, ba_refs, cum_refs, o_refs = (waf_ref, wab_ref), (baf_ref, bab_ref), (cumf_ref, cumb_ref), (of_ref, ob_ref)
    r = lax.broadcasted_iota(jnp.int32, (tb, tb), 0)
    c = lax.broadcasted_iota(jnp.int32, (tb, tb), 1)
    same = (r // GLA_CHUNK) == (c // GLA_CHUNK)
    mask = [same & (c <= r), same & (c >= r)]

    def gates(b, d):
        x = _dot(z_refs[d][b].astype(BF16), wa_refs[d][...]) + ba_refs[d][...]
        log_a = (jnp.minimum(x, 0.0) - jnp.log(1.0 + jnp.exp(-jnp.abs(x)))) * (1.0 / GLA_TAU)
        hi = log_a.astype(BF16)
        return hi, (log_a - hi.astype(F32)).astype(BF16)

    def decayed_operands(b, d, hi, lo):
        bcum = _dot(cum_refs[d][...], hi) + _dot(cum_refs[d][...], lo)
        edge = 0 if d else GLA_CHUNK - 1
        btot = jnp.concatenate(
            [jnp.broadcast_to(bcum[rows.start + edge:rows.start + edge + 1], (GLA_CHUNK, bcum.shape[1]))
             for rows in chunks], axis=0)
        k = k_refs[d][b]
        q_dec = ((q_refs[d][b] * GLA_HEAD_DIM ** -0.5) * jnp.exp(bcum)).astype(BF16)
        k_dec = (k * jnp.exp(-bcum)).astype(BF16)
        k_end = (k * jnp.exp(btot - bcum)).astype(BF16)
        return q_dec, k_dec, k_end, jnp.exp(btot), v_refs[d][b].astype(BF16)

    def attend(b, d, q_dec, k_dec, k_end, decay, vb):
        scores = [jnp.where(mask[d], _dot_nt(q_dec[:, hs], k_dec[:, hs]), 0.0).astype(BF16) for hs in heads]
        o_intra = [_dot(scores[h], vb[:, hs]) for h, hs in enumerate(heads)]
        kv = [[_dot_tn(vb[rows, hs], k_end[rows, hs]) for rows in chunks] for hs in heads]
        enter = [[None] * n_chunks for _ in heads]
        order = range(n_chunks - 1, -1, -1) if d else range(n_chunks)
        for h, hs in enumerate(heads):
            st = st_ref[b, d, h]
            for ci in order:
                enter[h][ci] = st.astype(BF16)
                st = decay[chunks[ci].start:chunks[ci].start + 1, hs] * st + kv[h][ci]
            st_ref[b, d, h] = st
        o_inter = [[_dot_nt(q_dec[rows, hs], enter[h][ci]) for ci, rows in enumerate(chunks)]
                   for h, hs in enumerate(heads)]
        o_refs[d][b] = jnp.concatenate(
            [o_intra[h] + jnp.concatenate(o_inter[h], axis=0) for h in range(GLA_HEADS)], axis=-1)

    units = [(b, d) for b in range(n_batch) for d in (0, 1)]
    split, ops = {}, {}
    for t in range(len(units) + 2):
        if t < len(units):
            split[t] = gates(*units[t])
        if 0 <= t - 1 < len(units):
            ops[t - 1] = decayed_operands(*units[t - 1], *split.pop(t - 1))
        if 0 <= t - 2 < len(units):
            attend(*units[t - 2], *ops.pop(t - 2))


def _gla_gate_params(w_alpha, b_alpha):
    depth = w_alpha.shape[0]
    r = GLA_LOWRANK
    zeros = jnp.zeros((depth, LANE, GLA_WIDTH), F32)
    waf = zeros.at[:, :r].set(w_alpha[:, 0]).astype(BF16)
    wab = zeros.at[:, r:2 * r].set(w_alpha[:, 1]).astype(BF16)
    return waf, wab, b_alpha[:, 0].reshape(depth, 1, GLA_WIDTH), b_alpha[:, 1].reshape(depth, 1, GLA_WIDTH)


def _gla(proj, gate_params, layer, *, tb=256):
    b, l, _ = proj.shape
    nblk = l // tb
    idx = np.arange(tb)
    same = (idx[:, None] // GLA_CHUNK) == (idx[None, :] // GLA_CHUNK)
    cum_f = jnp.asarray(same & (idx[None, :] <= idx[:, None]), BF16)
    cum_b = jnp.asarray(same & (idx[None, :] >= idx[:, None]), BF16)

    def col(base, width, rev):
        if rev:
            return pl.BlockSpec((b, tb, width), lambda i: (0, nblk - 1 - i, base // width))
        return pl.BlockSpec((b, tb, width), lambda i: (0, i, base // width))

    w = GLA_WIDTH
    in_specs = [col(COL_GQ, w, False), col(COL_GK, w, False), col(COL_GV, w, False), col(COL_Z, LANE, False),
                col(COL_GQ, w, True), col(COL_GK, w, True), col(COL_GV, w, True), col(COL_Z, LANE, True),
                _layer((LANE, w), layer), _layer((LANE, w), layer), _layer((1, w), layer), _layer((1, w), layer),
                _resident((tb, tb)), _resident((tb, tb))]
    return pl.pallas_call(
        _gla_body,
        grid=(nblk,),
        in_specs=in_specs,
        out_specs=[col(0, w, False), col(0, w, True)],
        out_shape=[jax.ShapeDtypeStruct((b, l, GLA_WIDTH), F32)] * 2,
        scratch_shapes=[pltpu.VMEM((b, 2, GLA_HEADS, GLA_HEAD_DIM, GLA_HEAD_DIM), F32)],
        compiler_params=_cparams(("arbitrary",), 48),
        name="gla",
    )(*([proj] * 8), *gate_params, cum_f, cum_b)


def _s5_expand(comb_ref, winc_ref, woc_ref, tz_scr, win_scr, wout_scr):
    tc, lg, h, p = S5_CHUNK, S5_LANE_GROUPS, S5_GROUP_CH, S5_STATE
    nst = lg * p
    row_g = lax.broadcasted_iota(jnp.int32, (LANE, 1), 0) // h
    same = row_g == lax.broadcasted_iota(jnp.int32, (1, LANE), 1) // h
    blocks = [jnp.where(same, jnp.concatenate([comb_ref[lag]] * lg, axis=0), 0.0).astype(BF16)
              for lag in range(2 * tc - 1)]
    for tp in range(tc):
        for t in range(tc):
            tz_scr[tp * LANE:(tp + 1) * LANE, t * LANE:(t + 1) * LANE] = blocks[t - tp + tc - 1]
    same = row_g == (lax.broadcasted_iota(jnp.int32, (1, 4 * nst), 1) % nst) // p
    for tp in range(tc):
        slab = winc_ref[tp * h:(tp + 1) * h, :]
        win_scr[tp * LANE:(tp + 1) * LANE, :] = jnp.where(
            same, jnp.concatenate([slab] * lg, axis=0), 0.0).astype(BF16)
    lane_g = (lax.broadcasted_iota(jnp.int32, (1, tc * LANE), 1) // h) % lg
    for part in range(4):
        tab = woc_ref[part * p:(part + 1) * p, :]
        for g in range(lg):
            r0 = (part * lg + g) * p
            wout_scr[r0:r0 + p, :] = jnp.where(lane_g == g, tab, 0.0).astype(BF16)


def _s5_body(u_ref, comb_ref, winc_ref, woc_ref, a_ref, d_ref, o_ref, tz_scr, win_scr, wout_scr, z_scr, x_scr):
    @pl.when(pl.program_id(1) == 0)
    def _():
        _s5_expand(comb_ref, winc_ref, woc_ref, tz_scr, win_scr, wout_scr)

    l = u_ref.shape[1]
    nc = l // S5_CHUNK
    nst = S5_LANE_GROUPS * S5_STATE
    uk = jnp.concatenate([u_ref[0, pl.ds(t, nc, stride=S5_CHUNK), :] for t in range(S5_CHUNK)], axis=-1)
    ukb = uk.astype(BF16)
    z_scr[...] = _dot(ukb, win_scr[...])
    y_local = _dot(ukb, tz_scr[...]) + uk * d_ref[...]

    a = a_ref[...]
    afr, afi, abr, abi = (a[:, i * nst:(i + 1) * nst] for i in range(4))
    ntile = nc // 8

    def tile_step(i, carry):
        sfr, sfi, sbr, sbi = carry
        rf = pl.multiple_of(i * 8, 8)
        rb = pl.multiple_of((ntile - 1 - i) * 8, 8)
        zf = z_scr[pl.ds(rf, 8), 0:2 * nst]
        zb = z_scr[pl.ds(rb, 8), 2 * nst:4 * nst]
        xfr, xfi, xbr, xbi = [], [], [None] * 8, [None] * 8
        for r in range(8):
            xfr.append(sfr)
            xfi.append(sfi)
            sfr, sfi = (afr * sfr - afi * sfi + zf[r:r + 1, :nst],
                        afr * sfi + afi * sfr + zf[r:r + 1, nst:])
            q = 7 - r
            xbr[q] = sbr
            xbi[q] = sbi
            sbr, sbi = (abr * sbr - abi * sbi + zb[q:q + 1, :nst],
                        abr * sbi + abi * sbr + zb[q:q + 1, nst:])
        x_scr[pl.ds(rf, 8), 0:nst] = jnp.concatenate(xfr, axis=0)
        x_scr[pl.ds(rf, 8), nst:2 * nst] = jnp.concatenate(xfi, axis=0)
        x_scr[pl.ds(rb, 8), 2 * nst:3 * nst] = jnp.concatenate(xbr, axis=0)
        x_scr[pl.ds(rb, 8), 3 * nst:4 * nst] = jnp.concatenate(xbi, axis=0)
        return sfr, sfi, sbr, sbi

    zero = jnp.zeros((1, nst), F32)
    lax.fori_loop(0, ntile, tile_step, (zero, zero, zero, zero), unroll=True)

    y = y_local + _dot(x_scr[...].astype(BF16), wout_scr[...])
    for t in range(S5_CHUNK):
        o_ref[0, pl.ds(t, nc, stride=S5_CHUNK), :] = y[:, t * LANE:(t + 1) * LANE]


def _s5(proj, tables, d_tiled, layer):
    b, l, _ = proj.shape
    comb, winc, woc, a = tables
    nk = S5_WIDTH // LANE
    nc = l // S5_CHUNK
    feat = S5_CHUNK * LANE
    nst4 = 4 * S5_LANE_GROUPS * S5_STATE
    per_block = lambda arr: pl.BlockSpec((None, None) + arr.shape[2:], lambda k, bi: (layer, k, 0, 0))
    return pl.pallas_call(
        _s5_body,
        grid=(nk, b),
        in_specs=[
            pl.BlockSpec((1, l, LANE), lambda k, bi: (bi, 0, COL_S5 // LANE + k)),
            pl.BlockSpec((None,) + comb.shape[1:3] + (LANE,), lambda k, bi: (layer, 0, 0, k)),
            per_block(winc), per_block(woc), per_block(a), per_block(d_tiled),
        ],
        out_specs=pl.BlockSpec((1, l, LANE), lambda k, bi: (bi, 0, k)),
        out_shape=jax.ShapeDtypeStruct((b, l, S5_WIDTH), F32),
        scratch_shapes=[pltpu.VMEM((feat, feat), BF16), pltpu.VMEM((feat, nst4), BF16), pltpu.VMEM((nst4, feat), BF16),
                        pltpu.VMEM((nc, nst4), F32), pltpu.VMEM((nc, nst4), F32)],
        compiler_params=_cparams(("arbitrary", "arbitrary"), 48),
        name="s5",
    )(proj, comb, winc, woc, a, d_tiled)


def _cmul(ar, ai, br, bi):
    return ar * br - ai * bi, ar * bi + ai * br


def _s5_tables(lam_re, lam_im, log_dt, b_re, b_im, c_re, c_im):
    g, p, h, tc, lg = S5_GROUPS, S5_STATE, S5_GROUP_CH, S5_CHUNK, S5_LANE_GROUPS
    nk = g // lg
    dt = jnp.exp(log_dt)[..., None]
    mag = jnp.exp(lam_re * dt)
    lbr, lbi = mag * jnp.cos(lam_im * dt), mag * jnp.sin(lam_im * dt)
    den = lam_re * lam_re + lam_im * lam_im
    nr, ni = lbr - 1.0, lbi
    fr, fi = (nr * lam_re + ni * lam_im) / den, (ni * lam_re - nr * lam_im) / den
    bbr, bbi = _cmul(fr[..., None], fi[..., None], b_re, b_im)
    pwr, pwi = [jnp.ones_like(lbr)], [jnp.zeros_like(lbi)]
    for _ in range(tc):
        nr_, ni_ = _cmul(pwr[-1], pwi[-1], lbr, lbi)
        pwr.append(nr_)
        pwi.append(ni_)
    pwr, pwi = jnp.stack(pwr, 1), jnp.stack(pwi, 1)

    kern = []
    for d in range(2):
        lanes = lambda x: jnp.repeat(x, h, axis=-1)
        cr = c_re[d].transpose(2, 0, 1).reshape(p, 1, 1, g * h)
        ci = c_im[d].transpose(2, 0, 1).reshape(p, 1, 1, g * h)
        pr = lanes(pwr[d, :tc].transpose(2, 0, 1)).reshape(p, tc, 1, g * h)
        pi = lanes(pwi[d, :tc].transpose(2, 0, 1)).reshape(p, tc, 1, g * h)
        br = lanes(bbr[d].transpose(1, 2, 0)).reshape(p, 1, h, g * h)
        bi = lanes(bbi[d].transpose(1, 2, 0)).reshape(p, 1, h, g * h)
        mr, mi = _cmul(cr, ci, pr, pi)
        kern.append(jnp.sum(mr * br - mi * bi, axis=0))
    comb = jnp.concatenate([kern[1][tc - 1:0:-1], (kern[0][0] + kern[1][0])[None], kern[0][1:tc]])

    parts = []
    for d, sel in ((0, np.arange(tc - 1, -1, -1)), (1, np.arange(tc))):
        r_, i_ = _cmul(pwr[d][sel][..., None], pwi[d][sel][..., None], bbr[d][None], bbi[d][None])
        parts += [r_, i_]
    w = jnp.stack([x.transpose(0, 3, 1, 2).reshape(tc, h, nk, lg * p) for x in parts])
    winc = w.transpose(3, 1, 2, 0, 4).reshape(nk, tc * h, 4 * lg * p)

    parts = []
    for d, sel in ((0, np.arange(1, tc + 1)), (1, np.arange(tc, 0, -1))):
        r_, i_ = _cmul(c_re[d][None], c_im[d][None], pwr[d][sel][:, :, None, :], pwi[d][sel][:, :, None, :])
        parts += [r_, -i_]
    wo = jnp.stack([x.reshape(tc, nk, lg, h, p).transpose(1, 4, 0, 2, 3).reshape(nk, p, tc * lg * h) for x in parts], 1)
    woc = wo.reshape(nk, 4 * p, tc * lg * h)

    a = jnp.stack([pwr[0, tc], pwi[0, tc], pwr[1, tc], pwi[1, tc]])
    a = a.reshape(4, nk, lg * p).transpose(1, 0, 2).reshape(nk, 1, 4 * lg * p)
    return comb, winc, woc, a


def _merge_body(x_ref, ys_ref, of_ref, ob_ref, gate_ref, ya_ref, mg_ref, wm_ref, bm_ref, wglu_ref,
                gg_ref, wbs_ref, wbg_ref, wba_ref, wo_ref, o_ref):
    x = x_ref[...]
    d = x.shape[1]
    h = _rms(x, mg_ref[...]).astype(BF16)

    g_lin = [_dot(h, wm_ref[:, i * d:(i + 1) * d].astype(BF16)) + bm_ref[:, i * d:(i + 1) * d] for i in range(3)]
    p_attn = _dot(ya_ref[...], wba_ref[...].astype(BF16))

    y = jax.nn.gelu(ys_ref[...])
    y_s5 = y * jax.nn.sigmoid(_dot(y.astype(BF16), wglu_ref[...].astype(BF16)))
    p_s5 = _dot(y_s5.astype(BF16), wbs_ref[...].astype(BF16))

    o = of_ref[...] + ob_ref[...]
    heads = []
    for hh in range(GLA_HEADS):
        oh = o[:, hh * GLA_HEAD_DIM:(hh + 1) * GLA_HEAD_DIM]
        heads.append(oh * lax.rsqrt(jnp.mean(oh * oh, axis=-1, keepdims=True) + NORM_EPS))
    gate = gate_ref[...]
    y_gla = jnp.concatenate(heads, axis=-1) * gg_ref[...] * (gate * jax.nn.sigmoid(gate))
    p_gla = _dot(y_gla.astype(BF16), wbg_ref[...].astype(BF16))

    merged = jax.nn.sigmoid(g_lin[0]) * p_s5 + jax.nn.sigmoid(g_lin[1]) * p_gla + jax.nn.sigmoid(g_lin[2]) * p_attn
    o_ref[...] = x + _dot(merged.astype(BF16), wo_ref[...].astype(BF16))


def _merge(x2d, ys5, o_f, o_b, proj2d, y_attn, mix_gain, w_merge, b_merge, w_glu, gla_gain,
           wb_s5, wb_gla, wb_attn, w_out, layer, *, tm=512):
    t, d = x2d.shape
    per_layer = lambda a: _layer(a.shape[1:], layer)
    row = lambda w: pl.BlockSpec((tm, w), lambda i: (i, 0))
    return pl.pallas_call(
        _merge_body,
        grid=(t // tm,),
        in_specs=[
            row(d), row(S5_WIDTH), row(GLA_WIDTH), row(GLA_WIDTH),
            pl.BlockSpec((tm, GLA_WIDTH), lambda i: (i, COL_GG // GLA_WIDTH)),
            row(ATTN_WIDTH),
            per_layer(mix_gain), per_layer(w_merge), per_layer(b_merge), per_layer(w_glu), per_layer(gla_gain),
            per_layer(wb_s5), per_layer(wb_gla), per_layer(wb_attn), per_layer(w_out),
        ],
        out_specs=row(d),
        out_shape=jax.ShapeDtypeStruct((t, d), F32),
        compiler_params=_cparams(("parallel",), 48),
        name="merge",
    )(x2d, ys5, o_f, o_b, proj2d, y_attn, mix_gain, w_merge, b_merge, w_glu, gla_gain,
      wb_s5, wb_gla, wb_attn, w_out)


def kernel(x, ffn1_norm, ffn1_w_gate, ffn1_w_up, ffn1_w_down, mix_norm, w_in, s5_lambda_re, s5_lambda_im, s5_log_dt, s5_b_re, s5_b_im, s5_c_re, s5_c_im, s5_d, s5_w_glu, gla_w_alpha, gla_b_alpha, gla_norm, attn_q_norm, attn_k_norm, w_branch_s5, w_branch_gla, w_branch_attn, w_merge_gate, b_merge_gate, w_out, ffn2_norm, ffn2_w_gate, ffn2_w_up, ffn2_w_down, final_norm):
    bsz, seq_len, d_model = x.shape
    depth = w_in.shape[0]
    tabs = _rope_tables(seq_len)
    vec = lambda g: g.reshape(depth, 1, -1)
    ffn1 = (vec(ffn1_norm), ffn1_w_gate, ffn1_w_up, ffn1_w_down)
    ffn2 = (vec(ffn2_norm), ffn2_w_gate, ffn2_w_up, ffn2_w_down)
    mix_gain = vec(mix_norm)
    s5_tabs = jax.vmap(_s5_tables)(s5_lambda_re, s5_lambda_im, s5_log_dt, s5_b_re, s5_b_im, s5_c_re, s5_c_im)
    s5_d_tiled = jnp.tile(s5_d.reshape(depth, S5_WIDTH // LANE, 1, LANE), (1, 1, 1, S5_CHUNK))
    gla_gates = _gla_gate_params(gla_w_alpha, gla_b_alpha)
    attn_gq = vec(jnp.tile(attn_q_norm, (1, ATTN_Q_HEADS)))
    attn_gk = vec(jnp.tile(attn_k_norm, (1, ATTN_KV_HEADS)))
    merge_params = (mix_gain, w_merge_gate, vec(b_merge_gate), s5_w_glu, vec(jnp.tile(gla_norm, (1, GLA_HEADS))),
                    w_branch_s5, w_branch_gla, w_branch_attn, w_out)

    flat = lambda a: a.reshape(bsz * seq_len, a.shape[-1])
    x2d = flat(x)
    for i in range(depth):
        x2d = _ffn(x2d, *ffn1, i)

        proj2d, qt, k, vt = _inproj(x2d, seq_len, mix_gain, w_in, tabs, attn_gq, attn_gk, i)
        proj = proj2d.reshape(bsz, seq_len, PROJ_WIDTH)
        ys5 = _s5(proj, s5_tabs, s5_d_tiled, i)
        o_f, o_b = _gla(proj, gla_gates, i)
        y_attn = _attention(qt, k, vt)
        x2d = _merge(x2d, flat(ys5), flat(o_f), flat(o_b), proj2d, flat(y_attn), *merge_params, i)

        x2d = _ffn(x2d, *ffn2, i, final_norm if i == depth - 1 else None)
    return x2d.reshape(bsz, seq_len, d_model)
```

```python
import functools
import math

import jax
import jax.numpy as jnp
import numpy as np
from jax import lax
from jax.experimental import pallas as pl
from jax.experimental.pallas import tpu as pltpu

F32 = jnp.float32
BF16 = jnp.bfloat16

NORM_EPS = 1e-6
S5_GROUPS = 32
S5_GROUP_CH = 16
S5_STATE = 64
S5_WIDTH = S5_GROUPS * S5_GROUP_CH
S5_CHUNK = 8
S5_LANE_GROUPS = 8
GLA_HEADS = 4
GLA_HEAD_DIM = 128
GLA_WIDTH = GLA_HEADS * GLA_HEAD_DIM
GLA_LOWRANK = 16
GLA_TAU = 16.0
GLA_CHUNK = 64
ATTN_Q_HEADS = 8
ATTN_KV_HEADS = 2
ATTN_HEAD_DIM = 64
ATTN_GROUP = ATTN_Q_HEADS // ATTN_KV_HEADS
ATTN_WIDTH = ATTN_Q_HEADS * ATTN_HEAD_DIM
ATTN_KV_WIDTH = ATTN_KV_HEADS * ATTN_HEAD_DIM
GRID_W = 64
ROPE_BASE = 10000.0
ROPE_PAIR = ATTN_HEAD_DIM // 4
ATTN_VT_ROWS = ATTN_HEAD_DIM + 16
ATTN_BOUND_MAX = 50.0
ATTN_BOUND_SLACK = 1.0 + 2.0 ** -10

LANE = 128
V7X_VMEM_BYTES = 64 * 1024 * 1024

COL_S5 = 0
COL_GQ = 512
COL_GK = 1024
COL_GV = 1536
COL_GG = 2048
COL_Z = 2560
PROJ_WIDTH = 2688


def _cparams(semantics, vmem_mib):
    return pltpu.CompilerParams(
        dimension_semantics=semantics,
        vmem_limit_bytes=min(vmem_mib * 1024 * 1024, V7X_VMEM_BYTES - 4 * 1024 * 1024),
    )


def _resident(shape):
    nd = len(shape)
    return pl.BlockSpec(shape, lambda *_: (0,) * nd, pipeline_mode=pl.Buffered(1))


def _layer(shape, layer):
    nd = len(shape)
    return pl.BlockSpec((None,) + tuple(shape), lambda *_: (layer,) + (0,) * nd, pipeline_mode=pl.Buffered(1))


def _rms(x, gain):
    ms = jnp.mean(x * x, axis=-1, keepdims=True)
    return x * lax.rsqrt(ms + NORM_EPS) * gain


def _dot(a, b):
    return jnp.dot(a, b, preferred_element_type=F32)


def _dot_nt(a, b):
    return lax.dot_general(a, b, (((1,), (1,)), ((), ())), preferred_element_type=F32)


def _dot_tn(a, b):
    return lax.dot_general(a, b, (((0,), (0,)), ((), ())), preferred_element_type=F32)


def _ffn_body(x_ref, gain_ref, wg_ref, wu_ref, wd_ref, *rest, chunk, final):
    if final:
        fg_ref, o_ref, a_ref = rest
    else:
        o_ref, a_ref = rest
    x = x_ref[...]
    h = _rms(x, gain_ref[...]).astype(BF16)
    d_ff = wg_ref.shape[1]
    for c0 in range(0, d_ff, chunk):
        g = _dot(h, wg_ref[:, c0:c0 + chunk].astype(BF16))
        u = _dot(h, wu_ref[:, c0:c0 + chunk].astype(BF16))
        a_ref[:, c0:c0 + chunk] = (g * jax.nn.sigmoid(g) * u).astype(BF16)
    out = x + 0.5 * _dot(a_ref[...], wd_ref[...].astype(BF16))
    if final:
        out = _rms(out, fg_ref[...])
    o_ref[...] = out


def _ffn(x2d, gain, wg, wu, wd, layer, final_gain=None, *, tm=512, chunk=256):
    t, d = x2d.shape
    f = wg.shape[2]
    final = final_gain is not None
    row = pl.BlockSpec((tm, d), lambda i: (i, 0))
    in_specs = [row, _layer((1, d), layer), _layer((d, f), layer), _layer((d, f), layer), _layer((f, d), layer)]
    args = [x2d, gain, wg, wu, wd]
    if final:
        in_specs.append(_resident((1, d)))
        args.append(final_gain.reshape(1, d))
    return pl.pallas_call(
        functools.partial(_ffn_body, chunk=chunk, final=final),
        grid=(t // tm,),
        in_specs=in_specs,
        out_specs=row,
        out_shape=jax.ShapeDtypeStruct((t, d), F32),
        scratch_shapes=[pltpu.VMEM((tm, f), BF16)],
        compiler_params=_cparams(("parallel",), 48),
        name="ffn",
    )(*args)


def _head_rms(x, ones_ref, gain):
    x2 = x * x
    hi = x2.astype(BF16)
    lo = (x2 - hi.astype(F32)).astype(BF16)
    ss = _dot(hi, ones_ref[...]) + _dot(lo, ones_ref[...])
    return x * lax.rsqrt(ss * (1.0 / ATTN_HEAD_DIM) + NORM_EPS) * gain


def _rope(x, cos, sin_signed):
    n = x.shape[-1]
    lane = lax.broadcasted_iota(jnp.int32, x.shape, 1)
    first = (lane & ROPE_PAIR) == 0
    partner = jnp.where(first, pltpu.roll(x, n - ROPE_PAIR, 1), pltpu.roll(x, ROPE_PAIR, 1))
    return x * cos + partner * sin_signed


def _attn_prep(q, kv, cq_ref, sq_ref, ck_ref, sk_ref, gq_ref, gk_ref, oq_ref, ok_ref, qt_ref, k_ref, vt_ref):
    qt_ref[0] = _rope(_head_rms(q, oq_ref, gq_ref[...]), cq_ref[...], sq_ref[...]).T.astype(BF16)
    k = _rope(_head_rms(kv[:, :ATTN_KV_WIDTH], ok_ref, gk_ref[...]), ck_ref[...], sk_ref[...])
    for g in range(ATTN_KV_HEADS):
        k_ref[0, g] = k[:, g * ATTN_HEAD_DIM:(g + 1) * ATTN_HEAD_DIM].astype(BF16)
    vt = kv[:, ATTN_KV_WIDTH:].T.astype(BF16)
    pad = ATTN_VT_ROWS - ATTN_HEAD_DIM
    ones_row = (lax.broadcasted_iota(jnp.int32, (pad, vt.shape[1]), 0) == 0).astype(F32).astype(BF16)
    for g in range(ATTN_KV_HEADS):
        vt_ref[0, g, :ATTN_HEAD_DIM, :] = vt[g * ATTN_HEAD_DIM:(g + 1) * ATTN_HEAD_DIM]
        vt_ref[0, g, ATTN_HEAD_DIM:, :] = ones_row


def _inproj_body(x_ref, gain_ref, w_ref, cq_ref, sq_ref, ck_ref, sk_ref, gq_ref, gk_ref, oq_ref, ok_ref,
                 o_ref, qt_ref, k_ref, vt_ref, *, chunk):
    h = _rms(x_ref[...], gain_ref[...]).astype(BF16)
    tail = _dot(h, w_ref[:, COL_Z:].astype(BF16))
    a0 = 2 * GLA_LOWRANK
    q = tail[:, a0:a0 + ATTN_WIDTH]
    kv = tail[:, a0 + ATTN_WIDTH:]
    _attn_prep(q, kv, cq_ref, sq_ref, ck_ref, sk_ref, gq_ref, gk_ref, oq_ref, ok_ref, qt_ref, k_ref, vt_ref)
    o_ref[:, COL_Z:] = tail[:, :PROJ_WIDTH - COL_Z]
    for c0 in range(0, COL_Z, chunk):
        o_ref[:, c0:c0 + chunk] = _dot(h, w_ref[:, c0:c0 + chunk].astype(BF16))


def _inproj(x2d, seq_len, gain, w, tabs, gq, gk, layer, *, tm=512, chunk=512):
    t, d = x2d.shape
    b, nb = t // seq_len, seq_len // tm
    cq, sq, ck, sk = tabs
    ones_q = jnp.asarray(np.kron(np.eye(ATTN_Q_HEADS), np.ones((ATTN_HEAD_DIM, ATTN_HEAD_DIM))), BF16)
    ones_k = jnp.asarray(np.kron(np.eye(ATTN_KV_HEADS), np.ones((ATTN_HEAD_DIM, ATTN_HEAD_DIM))), BF16)
    tab_q = pl.BlockSpec((tm, ATTN_WIDTH), lambda i: (i % nb, 0))
    tab_k = pl.BlockSpec((tm, ATTN_KV_WIDTH), lambda i: (i % nb, 0))
    return pl.pallas_call(
        functools.partial(_inproj_body, chunk=chunk),
        grid=(t // tm,),
        in_specs=[
            pl.BlockSpec((tm, d), lambda i: (i, 0)), _layer((1, d), layer), _layer(w.shape[1:], layer),
            tab_q, tab_q, tab_k, tab_k,
            _layer((1, ATTN_WIDTH), layer), _layer((1, ATTN_KV_WIDTH), layer),
            _resident((ATTN_WIDTH, ATTN_WIDTH)), _resident((ATTN_KV_WIDTH, ATTN_KV_WIDTH)),
        ],
        out_specs=[
            pl.BlockSpec((tm, PROJ_WIDTH), lambda i: (i, 0)),
            pl.BlockSpec((1, ATTN_WIDTH, tm), lambda i: (i // nb, 0, i % nb)),
            pl.BlockSpec((1, ATTN_KV_HEADS, tm, ATTN_HEAD_DIM), lambda i: (i // nb, 0, i % nb, 0)),
            pl.BlockSpec((1, ATTN_KV_HEADS, ATTN_VT_ROWS, tm), lambda i: (i // nb, 0, 0, i % nb)),
        ],
        out_shape=[
            jax.ShapeDtypeStruct((t, PROJ_WIDTH), F32),
            jax.ShapeDtypeStruct((b, ATTN_WIDTH, seq_len), BF16),
            jax.ShapeDtypeStruct((b, ATTN_KV_HEADS, seq_len, ATTN_HEAD_DIM), BF16),
            jax.ShapeDtypeStruct((b, ATTN_KV_HEADS, ATTN_VT_ROWS, seq_len), BF16),
        ],
        compiler_params=_cparams(("parallel",), 48),
        name="inproj",
    )(x2d, gain, w, cq, sq, ck, sk, gq, gk, ones_q, ones_k)


def _rope_tables(l):
    t = jnp.arange(l, dtype=jnp.int32)
    rows = (t // GRID_W).astype(F32)
    cols = (t % GRID_W).astype(F32)
    half = ATTN_HEAD_DIM // 2
    inv_freq = ROPE_BASE ** (-jnp.arange(half // 2, dtype=F32) * 2.0 / half)
    d = np.arange(ATTN_HEAD_DIM)
    use_rows = jnp.asarray((d // half) == 0)
    freq = inv_freq[jnp.asarray(d % (half // 2))]
    second = jnp.asarray((d % half) >= half // 2)
    ang = jnp.where(use_rows[None, :], rows[:, None], cols[:, None]) * freq[None, :]
    cos = jnp.cos(ang)
    sin = jnp.sin(ang)
    sin = jnp.where(second[None, :], sin, -sin)
    scale = ATTN_HEAD_DIM ** -0.5 * math.log2(math.e)
    return (jnp.tile(cos, (1, ATTN_Q_HEADS)) * scale, jnp.tile(sin, (1, ATTN_Q_HEADS)) * scale,
            jnp.tile(cos, (1, ATTN_KV_HEADS)), jnp.tile(sin, (1, ATTN_KV_HEADS)))


def _attn_body(qt_ref, k_ref, vt_ref, o_ref, kmax_scr, *, kb):
    n_kb = k_ref.shape[2] // kb
    items = [(h, j) for h in range(ATTN_GROUP) for j in range(n_kb)]
    heads = [slice(h * ATTN_HEAD_DIM, (h + 1) * ATTN_HEAD_DIM) for h in range(ATTN_GROUP)]

    def scores(h, j):
        return _dot(k_ref[0, 0, j * kb:(j + 1) * kb, :], qt_ref[0, heads[h], :])

    def values(j, p):
        return _dot(vt_ref[0, 0, :, j * kb:(j + 1) * kb], p.astype(BF16))

    def finish(accs):
        outs = [(a[:ATTN_HEAD_DIM] / a[ATTN_HEAD_DIM:ATTN_HEAD_DIM + 1]).T for a in accs]
        o_ref[0] = jnp.concatenate(outs, axis=-1).astype(o_ref.dtype)

    @pl.when(pl.program_id(2) == 0)
    def _():
        kf = k_ref[0, 0].astype(F32)
        k_max2 = jnp.max(jnp.sum(kf * kf, axis=-1, keepdims=True), axis=0, keepdims=True)
        kmax_scr[...] = jnp.broadcast_to(k_max2, kmax_scr.shape)

    k_max2 = kmax_scr[0:1, 0:1]
    qf = qt_ref[0].astype(F32)
    bounds = [jnp.sqrt(jnp.sum(qf[hs] * qf[hs], axis=0, keepdims=True) * k_max2) * ATTN_BOUND_SLACK
              for hs in heads]
    worst = bounds[0]
    for b in bounds[1:]:
        worst = jnp.maximum(worst, b)
    bounded = jnp.max(worst) <= ATTN_BOUND_MAX

    @pl.when(bounded)
    def _():
        accs = [None] * ATTN_GROUP
        st = scores(*items[0])
        for idx, (h, j) in enumerate(items):
            st_next = scores(*items[idx + 1]) if idx + 1 < len(items) else None
            blk = values(j, jnp.exp2(st - bounds[h]))
            accs[h] = blk if j == 0 else accs[h] + blk
            st = st_next
        finish(accs)

    @pl.when(jnp.logical_not(bounded))
    def _():
        accs = [None] * ATTN_GROUP
        st = scores(*items[0])
        run_max = None
        for idx, (h, j) in enumerate(items):
            st_next = scores(*items[idx + 1]) if idx + 1 < len(items) else None
            blk_max = jnp.max(st, axis=0, keepdims=True)
            new_max = blk_max if j == 0 else jnp.maximum(run_max, blk_max)
            blk = values(j, jnp.exp2(st - new_max))
            accs[h] = blk if j == 0 else jnp.exp2(run_max - new_max) * accs[h] + blk
            run_max = new_max
            st = st_next
        finish(accs)


def _attention(qt, k, vt, *, tq=1024, kb=256):
    b, _, l = qt.shape
    tq, kb = min(tq, l), min(kb, l)
    assert l % tq == 0 and l % kb == 0
    gw = ATTN_GROUP * ATTN_HEAD_DIM
    return pl.pallas_call(
        functools.partial(_attn_body, kb=kb),
        grid=(b, ATTN_KV_HEADS, l // tq),
        in_specs=[
            pl.BlockSpec((1, gw, tq), lambda bi, g, i: (bi, g, i)),
            pl.BlockSpec((1, 1, l, ATTN_HEAD_DIM), lambda bi, g, i: (bi, g, 0, 0)),
            pl.BlockSpec((1, 1, ATTN_VT_ROWS, l), lambda bi, g, i: (bi, g, 0, 0)),
        ],
        out_specs=pl.BlockSpec((1, tq, gw), lambda bi, g, i: (bi, i, g)),
        out_shape=jax.ShapeDtypeStruct((b, l, ATTN_WIDTH), BF16),
        scratch_shapes=[pltpu.VMEM((8, LANE), F32)],
        compiler_params=_cparams(("parallel", "parallel", "arbitrary"), 48),
        name="attention",
    )(qt, k, vt)


def _gla_body(qf_ref, kf_ref, vf_ref, zf_ref, qb_ref, kb_ref, vb_ref, zb_ref,
              waf_ref, wab_ref, baf_ref, bab_ref, cumf_ref, cumb_ref,
              of_ref, ob_ref, st_ref):
    @pl.when(pl.program_id(0) == 0)
    def _():
        st_ref[...] = jnp.zeros_like(st_ref)

    n_batch, tb = qf_ref.shape[0], qf_ref.shape[1]
    n_chunks = tb // GLA_CHUNK
    heads = [slice(h * GLA_HEAD_DIM, (h + 1) * GLA_HEAD_DIM) for h in range(GLA_HEADS)]
    chunks = [slice(ci * GLA_CHUNK, (ci + 1) * GLA_CHUNK) for ci in range(n_chunks)]
    q_refs, k_refs, v_refs, z_refs = (qf_ref, qb_ref), (kf_ref, kb_ref), (vf_ref, vb_ref), (zf_ref, zb_ref)
    wa_refs, ba_refs, cum_refs, o_refs = (waf_ref, wab_ref), (baf_ref, bab_ref), (cumf_ref, cumb_ref), (of_ref, ob_ref)
    r = lax.broadcasted_iota(jnp.int32, (tb, tb), 0)
    c = lax.broadcasted_iota(jnp.int32, (tb, tb), 1)
    same = (r // GLA_CHUNK) == (c // GLA_CHUNK)
    mask = [same & (c <= r), same & (c >= r)]

    def gates(b, d):
        x = _dot(z_refs[d][b].astype(BF16), wa_refs[d][...]) + ba_refs[d][...]
        log_a = (jnp.minimum(x, 0.0) - jnp.log(1.0 + jnp.exp(-jnp.abs(x)))) * (1.0 / GLA_TAU)
        hi = log_a.astype(BF16)
        return hi, (log_a - hi.astype(F32)).astype(BF16)

    def decayed_operands(b, d, hi, lo):
        bcum = _dot(cum_refs[d][...], hi) + _dot(cum_refs[d][...], lo)
        edge = 0 if d else GLA_CHUNK - 1
        btot = jnp.concatenate(
            [jnp.broadcast_to(bcum[rows.start + edge:rows.start + edge + 1], (GLA_CHUNK, bcum.shape[1]))
             for rows in chunks], axis=0)
        k = k_refs[d][b]
        q_dec = ((q_refs[d][b] * GLA_HEAD_DIM ** -0.5) * jnp.exp(bcum)).astype(BF16)
        k_dec = (k * jnp.exp(-bcum)).astype(BF16)
        k_end = (k * jnp.exp(btot - bcum)).astype(BF16)
        return q_dec, k_dec, k_end, jnp.exp(btot), v_refs[d][b].astype(BF16)

    def attend(b, d, q_dec, k_dec, k_end, decay, vb):
        scores = [jnp.where(mask[d], _dot_nt(q_dec[:, hs], k_dec[:, hs]), 0.0).astype(BF16) for hs in heads]
        o_intra = [_dot(scores[h], vb[:, hs]) for h, hs in enumerate(heads)]
        kv = [[_dot_tn(vb[rows, hs], k_end[rows, hs]) for rows in chunks] for hs in heads]
        enter = [[None] * n_chunks for _ in heads]
        order = range(n_chunks - 1, -1, -1) if d else range(n_chunks)
        for h, hs in enumerate(heads):
            st = st_ref[b, d, h]
            for ci in order:
                enter[h][ci] = st.astype(BF16)
                st = decay[chunks[ci].start:chunks[ci].start + 1, hs] * st + kv[h][ci]
            st_ref[b, d, h] = st
        o_inter = [[_dot_nt(q_dec[rows, hs], enter[h][ci]) for ci, rows in enumerate(chunks)]
                   for h, hs in enumerate(heads)]
        o_refs[d][b] = jnp.concatenate(
            [o_intra[h] + jnp.concatenate(o_inter[h], axis=0) for h in range(GLA_HEADS)], axis=-1)

    units = [(b, d) for b in range(n_batch) for d in (0, 1)]
    split, ops = {}, {}
    for t in range(len(units) + 2):
        if t < len(units):
            split[t] = gates(*units[t])
        if 0 <= t - 1 < len(units):
            ops[t - 1] = decayed_operands(*units[t - 1], *split.pop(t - 1))
        if 0 <= t - 2 < len(units):
            attend(*units[t - 2], *ops.pop(t - 2))


def _gla_gate_params(w_alpha, b_alpha):
    depth = w_alpha.shape[0]
    r = GLA_LOWRANK
    zeros = jnp.zeros((depth, LANE, GLA_WIDTH), F32)
    waf = zeros.at[:, :r].set(w_alpha[:, 0]).astype(BF16)
    wab = zeros.at[:, r:2 * r].set(w_alpha[:, 1]).astype(BF16)
    return waf, wab, b_alpha[:, 0].reshape(depth, 1, GLA_WIDTH), b_alpha[:, 1].reshape(depth, 1, GLA_WIDTH)


def _gla(proj, gate_params, layer, *, tb=128):
    b, l, _ = proj.shape
    nblk = l // tb
    idx = np.arange(tb)
    same = (idx[:, None] // GLA_CHUNK) == (idx[None, :] // GLA_CHUNK)
    cum_f = jnp.asarray(same & (idx[None, :] <= idx[:, None]), BF16)
    cum_b = jnp.asarray(same & (idx[None, :] >= idx[:, None]), BF16)

    def col(base, width, rev):
        if rev:
            return pl.BlockSpec((b, tb, width), lambda i: (0, nblk - 1 - i, base // width))
        return pl.BlockSpec((b, tb, width), lambda i: (0, i, base // width))

    w = GLA_WIDTH
    in_specs = [col(COL_GQ, w, False), col(COL_GK, w, False), col(COL_GV, w, False), col(COL_Z, LANE, False),
                col(COL_GQ, w, True), col(COL_GK, w, True), col(COL_GV, w, True), col(COL_Z, LANE, True),
                _layer((LANE, w), layer), _layer((LANE, w), layer), _layer((1, w), layer), _layer((1, w), layer),
                _resident((tb, tb)), _resident((tb, tb))]
    return pl.pallas_call(
        _gla_body,
        grid=(nblk,),
        in_specs=in_specs,
        out_specs=[col(0, w, False), col(0, w, True)],
        out_shape=[jax.ShapeDtypeStruct((b, l, GLA_WIDTH), F32)] * 2,
        scratch_shapes=[pltpu.VMEM((b, 2, GLA_HEADS, GLA_HEAD_DIM, GLA_HEAD_DIM), F32)],
        compiler_params=_cparams(("arbitrary",), 48),
        name="gla",
    )(*([proj] * 8), *gate_params, cum_f, cum_b)


def _s5_expand(comb_ref, winc_ref, woc_ref, tz_scr, win_scr, wout_scr):
    tc, lg, h, p = S5_CHUNK, S5_LANE_GROUPS, S5_GROUP_CH, S5_STATE
    nst = lg * p
    row_g = lax.broadcasted_iota(jnp.int32, (LANE, 1), 0) // h
    same = row_g == lax.broadcasted_iota(jnp.int32, (1, LANE), 1) // h
    blocks = [jnp.where(same, jnp.concatenate([comb_ref[lag]] * lg, axis=0), 0.0).astype(BF16)
              for lag in range(2 * tc - 1)]
    for tp in range(tc):
        for t in range(tc):
            tz_scr[tp * LANE:(tp + 1) * LANE, t * LANE:(t + 1) * LANE] = blocks[t - tp + tc - 1]
    same = row_g == (lax.broadcasted_iota(jnp.int32, (1, 4 * nst), 1) % nst) // p
    for tp in range(tc):
        slab = winc_ref[tp * h:(tp + 1) * h, :]
        win_scr[tp * LANE:(tp + 1) * LANE, :] = jnp.where(
            same, jnp.concatenate([slab] * lg, axis=0), 0.0).astype(BF16)
    lane_g = (lax.broadcasted_iota(jnp.int32, (1, tc * LANE), 1) // h) % lg
    for part in range(4):
        tab = woc_ref[part * p:(part + 1) * p, :]
        for g in range(lg):
            r0 = (part * lg + g) * p
            wout_scr[r0:r0 + p, :] = jnp.where(lane_g == g, tab, 0.0).astype(BF16)


def _s5_body(u_ref, comb_ref, winc_ref, woc_ref, a_ref, d_ref, o_ref, tz_scr, win_scr, wout_scr, z_scr, x_scr):
    @pl.when(pl.program_id(1) == 0)
    def _():
        _s5_expand(comb_ref, winc_ref, woc_ref, tz_scr, win_scr, wout_scr)

    l = u_ref.shape[1]
    nc = l // S5_CHUNK
    nst = S5_LANE_GROUPS * S5_STATE
    uk = jnp.concatenate([u_ref[0, pl.ds(t, nc, stride=S5_CHUNK), :] for t in range(S5_CHUNK)], axis=-1)
    ukb = uk.astype(BF16)
    z_scr[...] = _dot(ukb, win_scr[...])
    y_local = _dot(ukb, tz_scr[...]) + uk * d_ref[...]

    a = a_ref[...]
    afr, afi, abr, abi = (a[:, i * nst:(i + 1) * nst] for i in range(4))
    ntile = nc // 8

    def tile_step(i, carry):
        sfr, sfi, sbr, sbi = carry
        rf = pl.multiple_of(i * 8, 8)
        rb = pl.multiple_of((ntile - 1 - i) * 8, 8)
        zf = z_scr[pl.ds(rf, 8), 0:2 * nst]
        zb = z_scr[pl.ds(rb, 8), 2 * nst:4 * nst]
        xfr, xfi, xbr, xbi = [], [], [None] * 8, [None] * 8
        for r in range(8):
            xfr.append(sfr)
            xfi.append(sfi)
            sfr, sfi = (afr * sfr - afi * sfi + zf[r:r + 1, :nst],
                        afr * sfi + afi * sfr + zf[r:r + 1, nst:])
            q = 7 - r
            xbr[q] = sbr
            xbi[q] = sbi
            sbr, sbi = (abr * sbr - abi * sbi + zb[q:q + 1, :nst],
                        abr * sbi + abi * sbr + zb[q:q + 1, nst:])
        x_scr[pl.ds(rf, 8), 0:nst] = jnp.concatenate(xfr, axis=0)
        x_scr[pl.ds(rf, 8), nst:2 * nst] = jnp.concatenate(xfi, axis=0)
        x_scr[pl.ds(rb, 8), 2 * nst:3 * nst] = jnp.concatenate(xbr, axis=0)
        x_scr[pl.ds(rb, 8), 3 * nst:4 * nst] = jnp.concatenate(xbi, axis=0)
        return sfr, sfi, sbr, sbi

    zero = jnp.zeros((1, nst), F32)
    lax.fori_loop(0, ntile, tile_step, (zero, zero, zero, zero), unroll=True)

    y = y_local + _dot(x_scr[...].astype(BF16), wout_scr[...])
    for t in range(S5_CHUNK):
        o_ref[0, pl.ds(t, nc, stride=S5_CHUNK), :] = y[:, t * LANE:(t + 1) * LANE]


def _s5(proj, tables, d_tiled, layer):
    b, l, _ = proj.shape
    comb, winc, woc, a = tables
    nk = S5_WIDTH // LANE
    nc = l // S5_CHUNK
    feat = S5_CHUNK * LANE
    nst4 = 4 * S5_LANE_GROUPS * S5_STATE
    per_block = lambda arr: pl.BlockSpec((None, None) + arr.shape[2:], lambda k, bi: (layer, k, 0, 0))
    return pl.pallas_call(
        _s5_body,
        grid=(nk, b),
        in_specs=[
            pl.BlockSpec((1, l, LANE), lambda k, bi: (bi, 0, COL_S5 // LANE + k)),
            pl.BlockSpec((None,) + comb.shape[1:3] + (LANE,), lambda k, bi: (layer, 0, 0, k)),
            per_block(winc), per_block(woc), per_block(a), per_block(d_tiled),
        ],
        out_specs=pl.BlockSpec((1, l, LANE), lambda k, bi: (bi, 0, k)),
        out_shape=jax.ShapeDtypeStruct((b, l, S5_WIDTH), F32),
        scratch_shapes=[pltpu.VMEM((feat, feat), BF16), pltpu.VMEM((feat, nst4), BF16), pltpu.VMEM((nst4, feat), BF16),
                        pltpu.VMEM((nc, nst4), F32), pltpu.VMEM((nc, nst4), F32)],
        compiler_params=_cparams(("arbitrary", "arbitrary"), 48),
        name="s5",
    )(proj, comb, winc, woc, a, d_tiled)


def _cmul(ar, ai, br, bi):
    return ar * br - ai * bi, ar * bi + ai * br


def _s5_tables(lam_re, lam_im, log_dt, b_re, b_im, c_re, c_im):
    g, p, h, tc, lg = S5_GROUPS, S5_STATE, S5_GROUP_CH, S5_CHUNK, S5_LANE_GROUPS
    nk = g // lg
    dt = jnp.exp(log_dt)[..., None]
    mag = jnp.exp(lam_re * dt)
    lbr, lbi = mag * jnp.cos(lam_im * dt), mag * jnp.sin(lam_im * dt)
    den = lam_re * lam_re + lam_im * lam_im
    nr, ni = lbr - 1.0, lbi
    fr, fi = (nr * lam_re + ni * lam_im) / den, (ni * lam_re - nr * lam_im) / den
    bbr, bbi = _cmul(fr[..., None], fi[..., None], b_re, b_im)
    pwr, pwi = [jnp.ones_like(lbr)], [jnp.zeros_like(lbi)]
    for _ in range(tc):
        nr_, ni_ = _cmul(pwr[-1], pwi[-1], lbr, lbi)
        pwr.append(nr_)
        pwi.append(ni_)
    pwr, pwi = jnp.stack(pwr, 1), jnp.stack(pwi, 1)

    kern = []
    for d in range(2):
        lanes = lambda x: jnp.repeat(x, h, axis=-1)
        cr = c_re[d].transpose(2, 0, 1).reshape(p, 1, 1, g * h)
        ci = c_im[d].transpose(2, 0, 1).reshape(p, 1, 1, g * h)
        pr = lanes(pwr[d, :tc].transpose(2, 0, 1)).reshape(p, tc, 1, g * h)
        pi = lanes(pwi[d, :tc].transpose(2, 0, 1)).reshape(p, tc, 1, g * h)
        br = lanes(bbr[d].transpose(1, 2, 0)).reshape(p, 1, h, g * h)
        bi = lanes(bbi[d].transpose(1, 2, 0)).reshape(p, 1, h, g * h)
        mr, mi = _cmul(cr, ci, pr, pi)
        kern.append(jnp.sum(mr * br - mi * bi, axis=0))
    comb = jnp.concatenate([kern[1][tc - 1:0:-1], (kern[0][0] + kern[1][0])[None], kern[0][1:tc]])

    parts = []
    for d, sel in ((0, np.arange(tc - 1, -1, -1)), (1, np.arange(tc))):
        r_, i_ = _cmul(pwr[d][sel][..., None], pwi[d][sel][..., None], bbr[d][None], bbi[d][None])
        parts += [r_, i_]
    w = jnp.stack([x.transpose(0, 3, 1, 2).reshape(tc, h, nk, lg * p) for x in parts])
    winc = w.transpose(3, 1, 2, 0, 4).reshape(nk, tc * h, 4 * lg * p)

    parts = []
    for d, sel in ((0, np.arange(1, tc + 1)), (1, np.arange(tc, 0, -1))):
        r_, i_ = _cmul(c_re[d][None], c_im[d][None], pwr[d][sel][:, :, None, :], pwi[d][sel][:, :, None, :])
        parts += [r_, -i_]
    wo = jnp.stack([x.reshape(tc, nk, lg, h, p).transpose(1, 4, 0, 2, 3).reshape(nk, p, tc * lg * h) for x in parts], 1)
    woc = wo.reshape(nk, 4 * p, tc * lg * h)

    a = jnp.stack([pwr[0, tc], pwi[0, tc], pwr[1, tc], pwi[1, tc]])
    a = a.reshape(4, nk, lg * p).transpose(1, 0, 2).reshape(nk, 1, 4 * lg * p)
    return comb, winc, woc, a


def _merge_body(x_ref, ys_ref, of_ref, ob_ref, gate_ref, ya_ref, mg_ref, wm_ref, bm_ref, wglu_ref,
                gg_ref, wbs_ref, wbg_ref, wba_ref, wo_ref, o_ref):
    x = x_ref[...]
    d = x.shape[1]
    h = _rms(x, mg_ref[...]).astype(BF16)

    g_lin = [_dot(h, wm_ref[:, i * d:(i + 1) * d].astype(BF16)) + bm_ref[:, i * d:(i + 1) * d] for i in range(3)]
    p_attn = _dot(ya_ref[...], wba_ref[...].astype(BF16))

    y = jax.nn.gelu(ys_ref[...])
    y_s5 = y * jax.nn.sigmoid(_dot(y.astype(BF16), wglu_ref[...].astype(BF16)))
    p_s5 = _dot(y_s5.astype(BF16), wbs_ref[...].astype(BF16))

    o = of_ref[...] + ob_ref[...]
    heads = []
    for hh in range(GLA_HEADS):
        oh = o[:, hh * GLA_HEAD_DIM:(hh + 1) * GLA_HEAD_DIM]
        heads.append(oh * lax.rsqrt(jnp.mean(oh * oh, axis=-1, keepdims=True) + NORM_EPS))
    gate = gate_ref[...]
    y_gla = jnp.concatenate(heads, axis=-1) * gg_ref[...] * (gate * jax.nn.sigmoid(gate))
    p_gla = _dot(y_gla.astype(BF16), wbg_ref[...].astype(BF16))

    merged = jax.nn.sigmoid(g_lin[0]) * p_s5 + jax.nn.sigmoid(g_lin[1]) * p_gla + jax.nn.sigmoid(g_lin[2]) * p_attn
    o_ref[...] = x + _dot(merged.astype(BF16), wo_ref[...].astype(BF16))


def _merge(x2d, ys5, o_f, o_b, proj2d, y_attn, mix_gain, w_merge, b_merge, w_glu, gla_gain,
           wb_s5, wb_gla, wb_attn, w_out, layer, *, tm=512):
    t, d = x2d.shape
    per_layer = lambda a: _layer(a.shape[1:], layer)
    row = lambda w: pl.BlockSpec((tm, w), lambda i: (i, 0))
    return pl.pallas_call(
        _merge_body,
        grid=(t // tm,),
        in_specs=[
            row(d), row(S5_WIDTH), row(GLA_WIDTH), row(GLA_WIDTH),
            pl.BlockSpec((tm, GLA_WIDTH), lambda i: (i, COL_GG // GLA_WIDTH)),
            row(ATTN_WIDTH),
            per_layer(mix_gain), per_layer(w_merge), per_layer(b_merge), per_layer(w_glu), per_layer(gla_gain),
            per_layer(wb_s5), per_layer(wb_gla), per_layer(wb_attn), per_layer(w_out),
        ],
        out_specs=row(d),
        out_shape=jax.ShapeDtypeStruct((t, d), F32),
        compiler_params=_cparams(("parallel",), 48),
        name="merge",
    )(x2d, ys5, o_f, o_b, proj2d, y_attn, mix_gain, w_merge, b_merge, w_glu, gla_gain,
      wb_s5, wb_gla, wb_attn, w_out)


def kernel(x, ffn1_norm, ffn1_w_gate, ffn1_w_up, ffn1_w_down, mix_norm, w_in, s5_lambda_re, s5_lambda_im, s5_log_dt, s5_b_re, s5_b_im, s5_c_re, s5_c_im, s5_d, s5_w_glu, gla_w_alpha, gla_b_alpha, gla_norm, attn_q_norm, attn_k_norm, w_branch_s5, w_branch_gla, w_branch_attn, w_merge_gate, b_merge_gate, w_out, ffn2_norm, ffn2_w_gate, ffn2_w_up, ffn2_w_down, final_norm):
    bsz, seq_len, d_model = x.shape
    depth = w_in.shape[0]
    tabs = _rope_tables(seq_len)
    vec = lambda g: g.reshape(depth, 1, -1)
    ffn1 = (vec(ffn1_norm), ffn1_w_gate, ffn1_w_up, ffn1_w_down)
    ffn2 = (vec(ffn2_norm), ffn2_w_gate, ffn2_w_up, ffn2_w_down)
    mix_gain = vec(mix_norm)
    s5_tabs = jax.vmap(_s5_tables)(s5_lambda_re, s5_lambda_im, s5_log_dt, s5_b_re, s5_b_im, s5_c_re, s5_c_im)
    s5_d_tiled = jnp.tile(s5_d.reshape(depth, S5_WIDTH // LANE, 1, LANE), (1, 1, 1, S5_CHUNK))
    gla_gates = _gla_gate_params(gla_w_alpha, gla_b_alpha)
    attn_gq = vec(jnp.tile(attn_q_norm, (1, ATTN_Q_HEADS)))
    attn_gk = vec(jnp.tile(attn_k_norm, (1, ATTN_KV_HEADS)))
    merge_params = (mix_gain, w_merge_gate, vec(b_merge_gate), s5_w_glu, vec(jnp.tile(gla_norm, (1, GLA_HEADS))),
                    w_branch_s5, w_branch_gla, w_branch_attn, w_out)

    flat = lambda a: a.reshape(bsz * seq_len, a.shape[-1])
    x2d = flat(x)
    for i in range(depth):
        x2d = _ffn(x2d, *ffn1, i)

        proj2d, qt, k, vt = _inproj(x2d, seq_len, mix_gain, w_in, tabs, attn_gq, attn_gk, i)
        proj = proj2d.reshape(bsz, seq_len, PROJ_WIDTH)
        ys5 = _s5(proj, s5_tabs, s5_d_tiled, i)
        o_f, o_b = _gla(proj, gla_gates, i)
        y_attn = _attention(qt, k, vt)
        x2d = _merge(x2d, flat(ys5), flat(o_f), flat(o_b), proj2d, flat(y_attn), *merge_params, i)

        x2d = _ffn(x2d, *ffn2, i, final_norm if i == depth - 1 else None)
    return x2d.reshape(bsz, seq_len, d_model)
```
